```python
import math
import jax, jax.numpy as jnp
from jax import lax
import numpy as np

D_MODEL = 1024
BATCH = 4
SEQ = 4096
DEPTH = 2

CHUNK = 64
Q_BLOCK = 128
A_WIDTH = D_MODEL // 2
A_GROUPS = 4
A_GROUP_DIM = A_WIDTH // A_GROUPS
A_BLOCK = 128
B_HEADS = 4
B_HEAD_DIM = 64
B_V_DIM = 2 * B_HEAD_DIM
B_QK = B_HEADS * 2 * B_HEAD_DIM
B_VW = B_HEADS * B_V_DIM
C_WIDTH = D_MODEL // 2
CONV_K = 3
N_BRANCH = 3
D_FF = 4 * D_MODEL
SPLIT_SIZES = [A_WIDTH, A_WIDTH, B_QK, B_QK, B_VW, C_WIDTH, C_WIDTH, C_WIDTH, N_BRANCH * D_MODEL]
N_IN = int(sum(SPLIT_SIZES))
SPLIT_IDX = [int(v) for v in np.cumsum(SPLIT_SIZES)[:-1]]
EPS = 1e-6

kernel_name = "hybrid_gmlp_diffattn_shortconv_block"


def rms_norm(x, w):
    xf = x.astype(jnp.float32)
    y = xf * lax.rsqrt(jnp.mean(xf * xf, axis=-1, keepdims=True) + EPS)
    return (y * w.astype(jnp.float32)).astype(x.dtype)


def alibi_slopes(n_heads):
    return 2.0 ** (-8.0 * jnp.arange(1, n_heads + 1, dtype=jnp.float32) / n_heads)


def gmlp_spatial_gating(a_u, a_v, vnorm_w, ws, bs):
    b, s, _ = a_u.shape
    u = jax.nn.gelu(a_u)
    v = rms_norm(jax.nn.gelu(a_v), vnorm_w)
    vb = v.reshape(b, s // A_BLOCK, A_BLOCK, A_GROUPS, A_GROUP_DIM)
    causal = jnp.tril(jnp.ones((A_BLOCK, A_BLOCK), dtype=ws.dtype))
    ws_c = ws * causal[None]
    sv = jnp.einsum('gts,bnsgc->bntgc', ws_c, vb) + bs.T[:, :, None]
    return u * sv.reshape(b, s, A_WIDTH)


def differential_attention(q, k, v, lam, slopes):
    s_len = q.shape[1]
    scale = B_HEAD_DIM ** -0.5
    outs = []
    for i in range(s_len // Q_BLOCK):
        kv_len = (i + 1) * Q_BLOCK
        q_blk = q[:, i * Q_BLOCK:kv_len]
        k_blk = k[:, :kv_len]
        v_blk = v[:, :kv_len]
        sc = jnp.einsum('bqhmd,bkhmd->bhmqk', q_blk, k_blk).astype(jnp.float32) * scale
        qpos = i * Q_BLOCK + jnp.arange(Q_BLOCK)
        kpos = jnp.arange(kv_len)
        dist = jnp.abs(qpos[:, None] - kpos[None, :]).astype(jnp.float32)
        bias = -slopes[:, None, None, None] * dist
        mask = (kpos[None, :] // CHUNK) <= (qpos[:, None] // CHUNK)
        p = jax.nn.softmax(jnp.where(mask, sc + bias, -jnp.inf), axis=-1)
        a = p[:, :, 0] - lam * p[:, :, 1]
        outs.append(jnp.einsum('bhqk,bkhe->bqhe', a.astype(v.dtype), v_blk))
    return jnp.concatenate(outs, axis=1)


def causal_depthwise_conv(z, w):
    kern = w.astype(z.dtype)[:, None, :]
    return lax.conv_general_dilated(z, kern, window_strides=(1,), padding=[(CONV_K - 1, 0)],
                                    dimension_numbers=('NWC', 'WIO', 'NWC'),
                                    feature_group_count=z.shape[-1])


def setup_inputs(seed: int = 0) -> dict:
    key = jax.random.key(seed)
    ks = jax.random.split(key, 24)
    L, D = DEPTH, D_MODEL
    nrm = lambda k, shape, sc: jax.random.normal(k, shape, jnp.float32) * sc
    tri = jnp.tril(jnp.ones((A_BLOCK, A_BLOCK), jnp.float32))
    return {
        "x": nrm(ks[0], (BATCH, SEQ, D), 1.0),
        "norm1_w": 1.0 + nrm(ks[1], (L, D), 0.02),
        "w_in": nrm(ks[2], (L, D, N_IN), D ** -0.5),
        "gate_b": nrm(ks[3], (L, N_BRANCH, D), 0.02),
        "a_vnorm_w": 1.0 + nrm(ks[4], (L, A_WIDTH), 0.02),
        "a_ws": nrm(ks[5], (L, A_GROUPS, A_BLOCK, A_BLOCK), 0.5 * A_BLOCK ** -0.5) * tri,
        "a_bs": 1.0 + nrm(ks[6], (L, A_GROUPS, A_BLOCK), 0.02),
        "b_qnorm_w": 1.0 + nrm(ks[7], (L, 2, B_HEAD_DIM), 0.02),
        "b_knorm_w": 1.0 + nrm(ks[8], (L, 2, B_HEAD_DIM), 0.02),
        "b_lam": nrm(ks[9], (L, 4, B_HEAD_DIM), 0.1),
        "b_subnorm_w": 1.0 + nrm(ks[10], (L, B_V_DIM), 0.02),
        "c_conv_w": nrm(ks[11], (L, CONV_K, C_WIDTH), CONV_K ** -0.5),
        "w_br_a": nrm(ks[12], (L, A_WIDTH, D), A_WIDTH ** -0.5),
        "w_br_b": nrm(ks[13], (L, B_VW, D), B_VW ** -0.5),
        "w_br_c": nrm(ks[14], (L, C_WIDTH, D), C_WIDTH ** -0.5),
        "w_out": nrm(ks[15], (L, D, D), D ** -0.5),
        "norm2_w": 1.0 + nrm(ks[16], (L, D), 0.02),
        "w_ff1": nrm(ks[17], (L, D, D_FF), D ** -0.5),
        "w_ff2": nrm(ks[18], (L, D_FF, D), D_FF ** -0.5),
    }


def reference(x, norm1_w, w_in, gate_b, a_vnorm_w, a_ws, a_bs, b_qnorm_w, b_knorm_w, b_lam,
              b_subnorm_w, c_conv_w, w_br_a, w_br_b, w_br_c, w_out, norm2_w, w_ff1, w_ff2):
    b, s, d = x.shape
    slopes = alibi_slopes(B_HEADS)
    for l in range(DEPTH):
        h = rms_norm(x, norm1_w[l])
        proj = h @ w_in[l]
        a_u, a_v, q, k, v, c_b, c_c, c_x, g_pre = jnp.split(proj, SPLIT_IDX, axis=-1)

        y_a = gmlp_spatial_gating(a_u, a_v, a_vnorm_w[l], a_ws[l], a_bs[l])

        q = rms_norm(q.reshape(b, s, B_HEADS, 2, B_HEAD_DIM), b_qnorm_w[l])
        k = rms_norm(k.reshape(b, s, B_HEADS, 2, B_HEAD_DIM), b_knorm_w[l])
        v = v.reshape(b, s, B_HEADS, B_V_DIM)
        lam_p = b_lam[l].astype(jnp.float32)
        lambda_init = 0.8 - 0.6 * math.exp(-0.3 * l)
        lam = (jnp.exp(jnp.sum(lam_p[0] * lam_p[1])) - jnp.exp(jnp.sum(lam_p[2] * lam_p[3]))
               + lambda_init)
        o = differential_attention(q, k, v, lam, slopes)
        o = rms_norm(o, b_subnorm_w[l]) * (1.0 - lambda_init)
        y_b = o.reshape(b, s, B_VW)

        y_c = c_b * causal_depthwise_conv(c_c * c_x, c_conv_w[l])

        gates = jax.nn.sigmoid((g_pre.reshape(b, s, N_BRANCH, d) + gate_b[l]).astype(jnp.float32)).astype(x.dtype)
        merged = (gates[:, :, 0] * (y_a @ w_br_a[l])
                  + gates[:, :, 1] * (y_b @ w_br_b[l])
                  + gates[:, :, 2] * (y_c @ w_br_c[l]))
        x = x + merged @ w_out[l]

        h2 = rms_norm(x, norm2_w[l])
        x = x + jnp.square(jax.nn.relu(h2 @ w_ff1[l])) @ w_ff2[l]
    return x
```

```python
import functools
import math

import jax
import jax.numpy as jnp
from jax import lax
from jax.experimental import pallas as pl
from jax.experimental.pallas import tpu as pltpu

D_MODEL = 1024
A_WIDTH = 512
A_GROUPS = 4
A_BLOCK = 128
B_HEADS = 4
B_HEAD_DIM = 64
B_V_DIM = 128
B_QK = 512
B_VW = 512
C_WIDTH = 512
CONV_K = 3
N_BRANCH = 3
D_FF = 4 * D_MODEL
CHUNK = 64
EPS = 1e-6
LOG2E = math.log2(math.e)
NEG_BIG = -1e30

TM_PROJ = 512
TQ = 256
TK = 256
TM_MLP = 512
FF_CHUNK = 1024
CARRY_ROWS = 8
VMEM_LIMIT = 56 * 1024 * 1024

BF16 = jnp.bfloat16
F32 = jnp.float32


def _dot(a, b):
    return jnp.dot(a, b, preferred_element_type=F32)


def _dot_nt(a, b):
    return lax.dot_general(a, b, (((1,), (1,)), ((), ())), preferred_element_type=F32)


def _rms(x, w):
    return x * lax.rsqrt(jnp.mean(x * x, axis=-1, keepdims=True) + EPS) * w


def _resident(shape):
    zeros = (0,) * len(shape)
    return pl.BlockSpec(shape, lambda *_: zeros, pipeline_mode=pl.Buffered(1))


def _proj_kernel(x_ref, n1w_ref, wa_ref, wk_ref, wqvt_ref, wc_ref, vnw_ref, ws_ref, bst_ref,
                 qnw_ref, knw_ref, cw_ref,
                 ya_ref, qt_ref, k_ref, vt_ref, yc_ref, zc_ref, *, tiles_per_seq):
    i = pl.program_id(0)
    tm = x_ref.shape[0]
    h = _rms(x_ref[...], n1w_ref[...]).astype(BF16)

    pa = _dot(h, wa_ref[...])
    u = jax.nn.gelu(pa[:, :A_WIDTH])
    v = _rms(jax.nn.gelu(pa[:, A_WIDTH:]), vnw_ref[...]).astype(BF16)
    r = lax.broadcasted_iota(jnp.int32, (A_BLOCK, A_BLOCK), 0)
    c = lax.broadcasted_iota(jnp.int32, (A_BLOCK, A_BLOCK), 1)
    tril = c <= r
    gd = A_WIDTH // A_GROUPS
    for g in range(A_GROUPS):
        ws = jnp.where(tril, ws_ref[g], 0.0).astype(BF16)
        bias = bst_ref[:, g:g + 1]
        for blk in range(tm // A_BLOCK):
            rows = slice(blk * A_BLOCK, (blk + 1) * A_BLOCK)
            cols = slice(g * gd, (g + 1) * gd)
            sv = _dot(ws, v[rows, cols]) + bias
            ya_ref[rows, cols] = (u[rows, cols] * sv).astype(BF16)

    pk = _dot(h, wk_ref[...])
    gr = lax.broadcasted_iota(jnp.int32, (256, 256), 0) // B_HEAD_DIM
    gc = lax.broadcasted_iota(jnp.int32, (256, 256), 1) // B_HEAD_DIM
    ones_bd = jnp.where(gr == gc, 1.0, 0.0).astype(BF16)
    for j in range(B_QK // 256):
        cols = slice(j * 256, (j + 1) * 256)
        t = pk[:, cols]
        ss = _dot((t * t).astype(BF16), ones_bd) * (1.0 / B_HEAD_DIM)
        k_ref[:, cols] = (t * lax.rsqrt(ss + EPS) * knw_ref[:, cols]).astype(BF16)

    pqv = _dot_nt(wqvt_ref[...], h)
    q3 = pqv[:B_QK].reshape(B_QK // B_HEAD_DIM, B_HEAD_DIM, tm)
    qs = jnp.mean(q3 * q3, axis=1, keepdims=True)
    qn = q3 * lax.rsqrt(qs + EPS) * qnw_ref[...].reshape(B_QK // B_HEAD_DIM, B_HEAD_DIM, 1)
    qt_ref[...] = (qn * (B_HEAD_DIM ** -0.5 * LOG2E)).reshape(B_QK, tm).astype(BF16)
    vt_ref[...] = pqv[B_QK:].astype(BF16)

    pc = _dot(h, wc_ref[...])
    z = pc[:, C_WIDTH:2 * C_WIDTH] * pc[:, 2 * C_WIDTH:]

    @pl.when(i % tiles_per_seq == 0)
    def _():
        zc_ref[...] = jnp.zeros_like(zc_ref)

    row = lax.broadcasted_iota(jnp.int32, z.shape, 0)
    last1 = zc_ref[CARRY_ROWS - 1:CARRY_ROWS, :]
    last2 = zc_ref[CARRY_ROWS - 2:CARRY_ROWS - 1, :]
    z1 = jnp.where(row == 0, last1, pltpu.roll(z, 1, axis=0))
    z2 = jnp.where(row == 0, last2, jnp.where(row == 1, last1, pltpu.roll(z, 2, axis=0)))
    conv = cw_ref[0:1, :] * z2 + cw_ref[1:2, :] * z1 + cw_ref[2:3, :] * z
    yc_ref[...] = (pc[:, :C_WIDTH] * conv).astype(BF16)
    zc_ref[...] = z[tm - CARRY_ROWS:, :]


def _proj_call(x2d, n1w, wa, wk, wqvt, wc, vnw, ws, bst, qnw, knw, cw, *, seq):
    m = x2d.shape[0]
    tm = TM_PROJ
    row_bf = lambda width: pl.BlockSpec((tm, width), lambda i: (i, 0))
    col_bf = lambda height: pl.BlockSpec((height, tm), lambda i: (0, i))
    return pl.pallas_call(
        functools.partial(_proj_kernel, tiles_per_seq=seq // tm),
        grid=(m // tm,),
        in_specs=[
            pl.BlockSpec((tm, D_MODEL), lambda i: (i, 0)),
            _resident(n1w.shape), _resident(wa.shape), _resident(wk.shape), _resident(wqvt.shape),
            _resident(wc.shape), _resident(vnw.shape), _resident(ws.shape), _resident(bst.shape),
            _resident(qnw.shape), _resident(knw.shape), _resident(cw.shape),
        ],
        out_specs=[row_bf(A_WIDTH), col_bf(B_QK), row_bf(B_QK), col_bf(B_VW), row_bf(C_WIDTH)],
        out_shape=[
            jax.ShapeDtypeStruct((m, A_WIDTH), BF16),
            jax.ShapeDtypeStruct((B_QK, m), BF16),
            jax.ShapeDtypeStruct((m, B_QK), BF16),
            jax.ShapeDtypeStruct((B_VW, m), BF16),
            jax.ShapeDtypeStruct((m, C_WIDTH), BF16),
        ],
        scratch_shapes=[pltpu.VMEM((CARRY_ROWS, C_WIDTH), F32)],
        compiler_params=pltpu.CompilerParams(
            dimension_semantics=("arbitrary",), vmem_limit_bytes=VMEM_LIMIT),
        name="in_proj",
    )(x2d, n1w, wa, wk, wqvt, wc, vnw, ws, bst, qnw, knw, cw)


def _attn_kernel(slopes_ref, qt_ref, k_ref, vt_ref, lam_ref, snw_ref, o_ref, off_ref, diag_ref,
                 *, lambda_init):
    hd = pl.program_id(1)
    i = pl.program_id(2)

    @pl.when(i == 0)
    def _():
        slope = slopes_ref[hd] * LOG2E
        r = lax.broadcasted_iota(jnp.int32, (TK, TQ), 0)
        c = lax.broadcasted_iota(jnp.int32, (TK, TQ), 1)
        rel = (r - c).astype(F32)
        off_ref[...] = slope * rel
        visible = (r // CHUNK) <= (c // CHUNK)
        diag_ref[...] = jnp.where(visible, -slope * jnp.abs(rel), NEG_BIG)

    qt = qt_ref[...]
    first = lax.broadcasted_iota(jnp.int32, qt.shape, 0) < B_HEAD_DIM
    zero = jnp.zeros_like(qt)
    q_maps = (jnp.where(first, qt, zero), jnp.where(first, zero, qt))

    def block(kb, vtb, table, shift, state):
        new_state = []
        for qm, (m_old, l_old, acc) in zip(q_maps, state):
            t = _dot(kb, qm) + table
            m_new = jnp.maximum(m_old, jnp.max(t, axis=0, keepdims=True) + shift)
            alpha = jnp.exp2(m_old - m_new)
            p = jnp.exp2(t - (m_new - shift))
            l_new = alpha * l_old + jnp.sum(p, axis=0, keepdims=True)
            acc = alpha * acc + _dot(vtb, p.astype(BF16))
            new_state.append((m_new, l_new, acc))
        return tuple(new_state)

    def off_diag(j, state):
        start = pl.multiple_of(j * TK, TK)
        shift = (slopes_ref[hd] * (LOG2E * TK)) * (j - i).astype(F32)
        return block(k_ref[pl.ds(start, TK), :], vt_ref[:, pl.ds(start, TK)], off_ref[...], shift, state)

    init = tuple((jnp.full((1, TQ), NEG_BIG, F32), jnp.zeros((1, TQ), F32),
                  jnp.zeros((B_V_DIM, TQ), F32)) for _ in range(2))
    state = lax.fori_loop(0, i, off_diag, init)
    start = pl.multiple_of(i * TK, TK)
    state = block(k_ref[pl.ds(start, TK), :], vt_ref[:, pl.ds(start, TK)], diag_ref[...], 0.0, state)

    lp = lam_ref[...]
    lam = (jnp.exp(jnp.sum(lp[0:1] * lp[1:2], axis=-1, keepdims=True))
           - jnp.exp(jnp.sum(lp[2:3] * lp[3:4], axis=-1, keepdims=True)) + lambda_init)
    (_, l1, a1), (_, l2, a2) = state
    ot = a1 / l1 - lam * (a2 / l2)
    o = _rms(ot.T, snw_ref[...]) * (1.0 - lambda_init)
    o_ref[...] = o.astype(BF16)


def _attn_call(slopes, qt, k, vt, lam_p, snw, *, batch, seq, lambda_init):
    m = k.shape[0]
    nq = seq // TQ
    return pl.pallas_call(
        functools.partial(_attn_kernel, lambda_init=lambda_init),
        grid=(batch, B_HEADS, nq),
        in_specs=[
            pl.BlockSpec(memory_space=pltpu.SMEM),
            pl.BlockSpec((B_V_DIM, TQ), lambda b, h, i: (h, b * nq + i)),
            pl.BlockSpec((seq, B_V_DIM), lambda b, h, i: (b, h)),
            pl.BlockSpec((B_V_DIM, seq), lambda b, h, i: (h, b)),
            pl.BlockSpec(lam_p.shape, lambda b, h, i: (0, 0)),
            pl.BlockSpec(snw.shape, lambda b, h, i: (0, 0)),
        ],
        out_specs=pl.BlockSpec((TQ, B_V_DIM), lambda b, h, i: (b * nq + i, h)),
        out_shape=jax.ShapeDtypeStruct((m, B_VW), BF16),
        scratch_shapes=[pltpu.VMEM((TK, TQ), F32), pltpu.VMEM((TK, TQ), F32)],
        compiler_params=pltpu.CompilerParams(
            dimension_semantics=("arbitrary", "arbitrary", "arbitrary"), vmem_limit_bytes=VMEM_LIMIT),
        name="diff_attn",
    )(slopes, qt, k, vt, lam_p, snw)


def _mlp_kernel(x_ref, ya_ref, yb_ref, yc_ref, n1w_ref, wg_ref, gb_ref, wba_ref, wbb_ref, wbc_ref,
                wo_ref, n2w_ref, w1_ref, w2_ref, o_ref):
    x = x_ref[...]
    h = _rms(x, n1w_ref[...]).astype(BF16)
    merged = None
    for br, (y_ref, w_ref) in enumerate(((ya_ref, wba_ref), (yb_ref, wbb_ref), (yc_ref, wbc_ref))):
        cols = slice(br * D_MODEL, (br + 1) * D_MODEL)
        gate = jax.nn.sigmoid(_dot(h, wg_ref[:, cols]) + gb_ref[br:br + 1, :])
        term = gate * _dot(y_ref[...], w_ref[...])
        merged = term if merged is None else merged + term
    x1 = x + _dot(merged.astype(BF16), wo_ref[...])
    h2 = _rms(x1, n2w_ref[...]).astype(BF16)
    acc = x1
    for cidx in range(D_FF // FF_CHUNK):
        cols = slice(cidx * FF_CHUNK, (cidx + 1) * FF_CHUNK)
        f = jnp.square(jnp.maximum(_dot(h2, w1_ref[:, cols]), 0.0)).astype(BF16)
        acc = acc + _dot(f, w2_ref[cols, :])
    o_ref[...] = acc


def _mlp_call(x2d, ya, yb, yc, n1w, wg, gb, wba, wbb, wbc, wo, n2w, w1, w2):
    m = x2d.shape[0]
    tm = TM_MLP
    rows = lambda width: pl.BlockSpec((tm, width), lambda i: (i, 0))
    return pl.pallas_call(
        _mlp_kernel,
        grid=(m // tm,),
        in_specs=[rows(D_MODEL), rows(A_WIDTH), rows(B_VW), rows(C_WIDTH),
                  _resident(n1w.shape), _resident(wg.shape), _resident(gb.shape),
                  _resident(wba.shape), _resident(wbb.shape), _resident(wbc.shape),
                  _resident(wo.shape), _resident(n2w.shape), _resident(w1.shape), _resident(w2.shape)],
        out_specs=rows(D_MODEL),
        out_shape=jax.ShapeDtypeStruct((m, D_MODEL), F32),
        compiler_params=pltpu.CompilerParams(
            dimension_semantics=("arbitrary",), vmem_limit_bytes=VMEM_LIMIT),
        name="merge_mlp",
    )(x2d, ya, yb, yc, n1w, wg, gb, wba, wbb, wbc, wo, n2w, w1, w2)


def kernel(x, norm1_w, w_in, gate_b, a_vnorm_w, a_ws, a_bs, b_qnorm_w, b_knorm_w, b_lam, b_subnorm_w,
           c_conv_w, w_br_a, w_br_b, w_br_c, w_out, norm2_w, w_ff1, w_ff2):
    batch, seq, d = x.shape
    depth = w_in.shape[0]
    slopes = 2.0 ** (-8.0 * jnp.arange(1, B_HEADS + 1, dtype=F32) / B_HEADS)
    o_a, o_q, o_k, o_v, o_c, o_g = 0, 1024, 1536, 2048, 2560, 4096
    x2d = x.reshape(batch * seq, d)
    for l in range(depth):
        w = w_in[l].astype(BF16)
        wa = w[:, o_a:o_q]
        wqvt = jnp.concatenate([w[:, o_q:o_k], w[:, o_v:o_c]], axis=1).T
        wk = w[:, o_k:o_v]
        wc = w[:, o_c:o_g]
        wg = w[:, o_g:]
        n1w = norm1_w[l].reshape(1, d)
        qnw = jnp.tile(b_qnorm_w[l].reshape(2 * B_HEAD_DIM), B_HEADS).reshape(B_QK, 1)
        knw = jnp.tile(b_knorm_w[l].reshape(2 * B_HEAD_DIM), B_HEADS).reshape(1, B_QK)
        lambda_init = 0.8 - 0.6 * math.exp(-0.3 * l)

        ya, qt, k, vt, yc = _proj_call(
            x2d, n1w, wa, wk, wqvt, wc, a_vnorm_w[l].reshape(1, A_WIDTH), a_ws[l], a_bs[l].T,
            qnw, knw, c_conv_w[l], seq=seq)
        yb = _attn_call(slopes, qt, k, vt, b_lam[l], b_subnorm_w[l].reshape(1, B_V_DIM),
                        batch=batch, seq=seq, lambda_init=lambda_init)
        x2d = _mlp_call(
            x2d, ya, yb, yc, n1w, wg, gate_b[l], w_br_a[l].astype(BF16), w_br_b[l].astype(BF16),
            w_br_c[l].astype(BF16), w_out[l].astype(BF16), norm2_w[l].reshape(1, d),
            w_ff1[l].astype(BF16), w_ff2[l].astype(BF16))
    return x2d.reshape(batch, seq, d)
```

```python
import functools
import math

import jax
import jax.numpy as jnp
from jax import lax
from jax.experimental import pallas as pl
from jax.experimental.pallas import tpu as pltpu

D_MODEL = 1024
A_WIDTH = 512
A_GROUPS = 4
A_BLOCK = 128
B_HEADS = 4
B_HEAD_DIM = 64
B_V_DIM = 128
B_QK = 512
B_VW = 512
C_WIDTH = 512
CONV_K = 3
N_BRANCH = 3
D_FF = 4 * D_MODEL
CHUNK = 64
EPS = 1e-6
LOG2E = math.log2(math.e)
NEG_BIG = -1e30

TM_PROJ = 512
TQ = 1024
SUB = 256
N_SUB = TQ // SUB
QK_AHEAD = 2
POS_RADIX = 32
SLOPE_PIECES = 3
TM_MLP = 512
FF_CHUNK = 1024
CARRY_ROWS = 8
VMEM_LIMIT = 56 * 1024 * 1024

BF16 = jnp.bfloat16
F32 = jnp.float32


def _dot(a, b):
    return jnp.dot(a, b, preferred_element_type=F32)


def _dot_nt(a, b):
    return lax.dot_general(a, b, (((1,), (1,)), ((), ())), preferred_element_type=F32)


def _rms(x, w):
    return x * lax.rsqrt(jnp.mean(x * x, axis=-1, keepdims=True) + EPS) * w


def _resident(shape):
    zeros = (0,) * len(shape)
    return pl.BlockSpec(shape, lambda *_: zeros, pipeline_mode=pl.Buffered(1))


def _proj_kernel(x_ref, n1w_ref, wa_ref, wk_ref, wqvt_ref, wc_ref, vnw_ref, ws_ref, bst_ref,
                 qnw_ref, knw_ref, cw_ref,
                 ya_ref, qt_ref, k_ref, vt_ref, yc_ref, zc_ref, *, tiles_per_seq):
    i = pl.program_id(0)
    tm = x_ref.shape[0]
    h = _rms(x_ref[...], n1w_ref[...]).astype(BF16)

    pa = _dot(h, wa_ref[...])
    u = jax.nn.gelu(pa[:, :A_WIDTH])
    v = _rms(jax.nn.gelu(pa[:, A_WIDTH:]), vnw_ref[...]).astype(BF16)
    r = lax.broadcasted_iota(jnp.int32, (A_BLOCK, A_BLOCK), 0)
    c = lax.broadcasted_iota(jnp.int32, (A_BLOCK, A_BLOCK), 1)
    tril = c <= r
    gd = A_WIDTH // A_GROUPS
    for g in range(A_GROUPS):
        ws = jnp.where(tril, ws_ref[g], 0.0).astype(BF16)
        bias = bst_ref[:, g:g + 1]
        for blk in range(tm // A_BLOCK):
            rows = slice(blk * A_BLOCK, (blk + 1) * A_BLOCK)
            cols = slice(g * gd, (g + 1) * gd)
            sv = _dot(ws, v[rows, cols]) + bias
            ya_ref[rows, cols] = (u[rows, cols] * sv).astype(BF16)

    pk = _dot(h, wk_ref[...])
    gr = lax.broadcasted_iota(jnp.int32, (256, 256), 0) // B_HEAD_DIM
    gc = lax.broadcasted_iota(jnp.int32, (256, 256), 1) // B_HEAD_DIM
    ones_bd = jnp.where(gr == gc, 1.0, 0.0).astype(BF16)
    for j in range(B_QK // 256):
        cols = slice(j * 256, (j + 1) * 256)
        t = pk[:, cols]
        ss = _dot((t * t).astype(BF16), ones_bd) * (1.0 / B_HEAD_DIM)
        k_ref[:, cols] = (t * lax.rsqrt(ss + EPS) * knw_ref[:, cols]).astype(BF16)

    pqv = _dot_nt(wqvt_ref[...], h)
    q3 = pqv[:B_QK].reshape(B_QK // B_HEAD_DIM, B_HEAD_DIM, tm)
    qs = jnp.mean(q3 * q3, axis=1, keepdims=True)
    qn = q3 * lax.rsqrt(qs + EPS) * qnw_ref[...].reshape(B_QK // B_HEAD_DIM, B_HEAD_DIM, 1)
    qt_ref[...] = (qn * (B_HEAD_DIM ** -0.5 * LOG2E)).reshape(B_QK, tm).astype(BF16)
    vt_ref[...] = pqv[B_QK:].astype(BF16)

    pc = _dot(h, wc_ref[...])
    z = pc[:, C_WIDTH:2 * C_WIDTH] * pc[:, 2 * C_WIDTH:]

    @pl.when(i % tiles_per_seq == 0)
    def _():
        zc_ref[...] = jnp.zeros_like(zc_ref)

    row = lax.broadcasted_iota(jnp.int32, z.shape, 0)
    last1 = zc_ref[CARRY_ROWS - 1:CARRY_ROWS, :]
    last2 = zc_ref[CARRY_ROWS - 2:CARRY_ROWS - 1, :]
    z1 = jnp.where(row == 0, last1, pltpu.roll(z, 1, axis=0))
    z2 = jnp.where(row == 0, last2, jnp.where(row == 1, last1, pltpu.roll(z, 2, axis=0)))
    conv = cw_ref[0:1, :] * z2 + cw_ref[1:2, :] * z1 + cw_ref[2:3, :] * z
    yc_ref[...] = (pc[:, :C_WIDTH] * conv).astype(BF16)
    zc_ref[...] = z[tm - CARRY_ROWS:, :]


def _proj_call(x2d, n1w, wa, wk, wqvt, wc, vnw, ws, bst, qnw, knw, cw, *, seq):
    m = x2d.shape[0]
    tm = TM_PROJ
    row_bf = lambda width: pl.BlockSpec((tm, width), lambda i: (i, 0))
    col_bf = lambda height: pl.BlockSpec((height, tm), lambda i: (0, i))
    return pl.pallas_call(
        functools.partial(_proj_kernel, tiles_per_seq=seq // tm),
        grid=(m // tm,),
        in_specs=[
            pl.BlockSpec((tm, D_MODEL), lambda i: (i, 0)),
            _resident(n1w.shape), _resident(wa.shape), _resident(wk.shape), _resident(wqvt.shape),
            _resident(wc.shape), _resident(vnw.shape), _resident(ws.shape), _resident(bst.shape),
            _resident(qnw.shape), _resident(knw.shape), _resident(cw.shape),
        ],
        out_specs=[row_bf(A_WIDTH), col_bf(B_QK), row_bf(B_QK), col_bf(B_VW), row_bf(C_WIDTH)],
        out_shape=[
            jax.ShapeDtypeStruct((m, A_WIDTH), BF16),
            jax.ShapeDtypeStruct((B_QK, m), BF16),
            jax.ShapeDtypeStruct((m, B_QK), BF16),
            jax.ShapeDtypeStruct((B_VW, m), BF16),
            jax.ShapeDtypeStruct((m, C_WIDTH), BF16),
        ],
        scratch_shapes=[pltpu.VMEM((CARRY_ROWS, C_WIDTH), F32)],
        compiler_params=pltpu.CompilerParams(
            dimension_semantics=("arbitrary",), vmem_limit_bytes=VMEM_LIMIT),
        name="in_proj",
    )(x2d, n1w, wa, wk, wqvt, wc, vnw, ws, bst, qnw, knw, cw)


def _attn_kernel(slopes_ref, qt_ref, k_ref, vt_ref, lam_ref, snw_ref, o_ref,
                 kfeat_ref, diag_ref, m_ref, l_ref, acc_ref, *, lambda_init):
    hd = pl.program_id(1)
    i = pl.program_id(2)
    slope = slopes_ref[hd] * LOG2E

    @pl.when(i == 0)
    def _():
        r = lax.broadcasted_iota(jnp.int32, (SUB, SUB), 0)
        c = lax.broadcasted_iota(jnp.int32, (SUB, SUB), 1)
        visible = (r // CHUNK) <= (c // CHUNK)
        diag_ref[...] = jnp.where(visible, jnp.minimum(2.0 * slope * (c - r).astype(F32), 0.0), NEG_BIG)
        kr = lax.broadcasted_iota(jnp.int32, kfeat_ref.shape, 0)
        kl = lax.broadcasted_iota(jnp.int32, kfeat_ref.shape, 1)
        feat = jnp.where(kl < SLOPE_PIECES, kr // POS_RADIX, kr % POS_RADIX)
        kfeat_ref[...] = jnp.where(kl < 2 * SLOPE_PIECES, feat, 0).astype(F32).astype(BF16)

    pieces = []
    rest = jnp.full((2 * B_HEAD_DIM, SUB), slope, F32)
    for _ in range(SLOPE_PIECES):
        piece = rest.astype(BF16).astype(F32)
        pieces.append(piece)
        rest = rest - piece
    frow = lax.broadcasted_iota(jnp.int32, (2 * B_HEAD_DIM, SUB), 0)
    piece = functools.reduce(lambda acc, n: jnp.where(frow % SLOPE_PIECES == n, pieces[n], acc),
                             range(1, SLOPE_PIECES), pieces[0])
    q_feat = jnp.where(frow < SLOPE_PIECES, float(POS_RADIX) * piece,
                       jnp.where(frow < 2 * SLOPE_PIECES, piece, 0.0)).astype(BF16)

    qt = qt_ref[...]
    first = lax.broadcasted_iota(jnp.int32, qt.shape, 0) < B_HEAD_DIM
    zero = jnp.zeros_like(qt)
    q_maps = (jnp.where(first, qt, zero), jnp.where(first, zero, qt))

    def super_block(key_start, shift, diagonal):
        def n_key_blocks(a):
            return a + 1 if diagonal else N_SUB

        def scores(ch):
            a, mp = divmod(ch, 2)
            rows = n_key_blocks(a) * SUB
            kk = jnp.concatenate([k_ref[pl.ds(key_start, rows), :], kfeat_ref[0:rows, :]], axis=1)
            qq = jnp.concatenate([q_maps[mp][:, a * SUB:(a + 1) * SUB], q_feat], axis=0)
            return _dot(kk, qq)

        def softmax_pv(ch, s):
            a = ch // 2
            nk = n_key_blocks(a)
            ts = [s[ks * SUB:(ks + 1) * SUB] for ks in range(nk)]
            if diagonal:
                ts[a] = ts[a] + diag_ref[...]
            mb = functools.reduce(jnp.maximum, [jnp.max(t, axis=0, keepdims=True) for t in ts])
            if diagonal:
                m_new = mb
            else:
                m_old = m_ref[ch]
                m_new = jnp.maximum(m_old, mb + shift)
                alpha = jnp.exp2(m_old - m_new)
            mm = m_new - shift
            ps = [jnp.exp2(t - mm) for t in ts]
            lsum = functools.reduce(jnp.add, [jnp.sum(p, axis=0, keepdims=True) for p in ps])
            pb = jnp.concatenate([p.astype(BF16) for p in ps], axis=0)
            pv = _dot(vt_ref[:, pl.ds(key_start, nk * SUB)], pb)
            if diagonal:
                l_ref[ch] = lsum
                acc_ref[ch] = pv
            else:
                l_ref[ch] = alpha * l_ref[ch] + lsum
                acc_ref[ch] = alpha * acc_ref[ch] + pv
            m_ref[ch] = m_new

        n_chains = 2 * N_SUB
        pending = [scores(ch) for ch in range(QK_AHEAD)]
        for ch in range(n_chains):
            if ch + QK_AHEAD < n_chains:
                pending.append(scores(ch + QK_AHEAD))
            softmax_pv(ch, pending.pop(0))

    super_block(pl.multiple_of(i * TQ, TQ), 0.0, True)

    def off_diag(j, carry):
        shift = (slope * float(TQ)) * (j - i).astype(F32)
        super_block(pl.multiple_of(j * TQ, TQ), shift, False)
        return carry

    lax.fori_loop(0, i, off_diag, 0)

    lp = lam_ref[...]
    lam = (jnp.exp(jnp.sum(lp[0:1] * lp[1:2], axis=-1, keepdims=True))
           - jnp.exp(jnp.sum(lp[2:3] * lp[3:4], axis=-1, keepdims=True)) + lambda_init)
    for a in range(N_SUB):
        ot = acc_ref[2 * a] / l_ref[2 * a] - lam * (acc_ref[2 * a + 1] / l_ref[2 * a + 1])
        o = _rms(ot.T, snw_ref[...]) * (1.0 - lambda_init)
        o_ref[a * SUB:(a + 1) * SUB, :] = o.astype(BF16)


def _attn_call(slopes, qt, k, vt, lam_p, snw, *, batch, seq, lambda_init):
    m = k.shape[0]
    nq = seq // TQ
    return pl.pallas_call(
        functools.partial(_attn_kernel, lambda_init=lambda_init),
        grid=(batch, B_HEADS, nq),
        in_specs=[
            pl.BlockSpec(memory_space=pltpu.SMEM),
            pl.BlockSpec((B_V_DIM, TQ), lambda b, h, i: (h, b * nq + i)),
            pl.BlockSpec((seq, B_V_DIM), lambda b, h, i: (b, h)),
            pl.BlockSpec((B_V_DIM, seq), lambda b, h, i: (h, b)),
            pl.BlockSpec(lam_p.shape, lambda b, h, i: (0, 0)),
            pl.BlockSpec(snw.shape, lambda b, h, i: (0, 0)),
        ],
        out_specs=pl.BlockSpec((TQ, B_V_DIM), lambda b, h, i: (b * nq + i, h)),
        out_shape=jax.ShapeDtypeStruct((m, B_VW), BF16),
        scratch_shapes=[
            pltpu.VMEM((TQ, 2 * B_HEAD_DIM), BF16),
            pltpu.VMEM((SUB, SUB), F32),
            pltpu.VMEM((2 * N_SUB, 1, SUB), F32),
            pltpu.VMEM((2 * N_SUB, 1, SUB), F32),
            pltpu.VMEM((2 * N_SUB, B_V_DIM, SUB), F32),
        ],
        compiler_params=pltpu.CompilerParams(
            dimension_semantics=("arbitrary", "arbitrary", "arbitrary"), vmem_limit_bytes=VMEM_LIMIT),
        name="diff_attn",
    )(slopes, qt, k, vt, lam_p, snw)


def _mlp_kernel(x_ref, ya_ref, yb_ref, yc_ref, n1w_ref, wg_ref, gb_ref, wba_ref, wbb_ref, wbc_ref,
                wo_ref, n2w_ref, w1_ref, w2_ref, o_ref):
    x = x_ref[...]
    h = _rms(x, n1w_ref[...]).astype(BF16)
    merged = None
    for br, (y_ref, w_ref) in enumerate(((ya_ref, wba_ref), (yb_ref, wbb_ref), (yc_ref, wbc_ref))):
        cols = slice(br * D_MODEL, (br + 1) * D_MODEL)
        gate = jax.nn.sigmoid(_dot(h, wg_ref[:, cols]) + gb_ref[br:br + 1, :])
        term = gate * _dot(y_ref[...], w_ref[...])
        merged = term if merged is None else merged + term
    x1 = x + _dot(merged.astype(BF16), wo_ref[...])
    h2 = _rms(x1, n2w_ref[...]).astype(BF16)
    acc = x1
    for cidx in range(D_FF // FF_CHUNK):
        cols = slice(cidx * FF_CHUNK, (cidx + 1) * FF_CHUNK)
        f = jnp.square(jnp.maximum(_dot(h2, w1_ref[:, cols]), 0.0)).astype(BF16)
        acc = acc + _dot(f, w2_ref[cols, :])
    o_ref[...] = acc


def _mlp_call(x2d, ya, yb, yc, n1w, wg, gb, wba, wbb, wbc, wo, n2w, w1, w2):
    m = x2d.shape[0]
    tm = TM_MLP
    rows = lambda width: pl.BlockSpec((tm, width), lambda i: (i, 0))
    return pl.pallas_call(
        _mlp_kernel,
        grid=(m // tm,),
        in_specs=[rows(D_MODEL), rows(A_WIDTH), rows(B_VW), rows(C_WIDTH),
                  _resident(n1w.shape), _resident(wg.shape), _resident(gb.shape),
                  _resident(wba.shape), _resident(wbb.shape), _resident(wbc.shape),
                  _resident(wo.shape), _resident(n2w.shape), _resident(w1.shape), _resident(w2.shape)],
        out_specs=rows(D_MODEL),
        out_shape=jax.ShapeDtypeStruct((m, D_MODEL), F32),
        compiler_params=pltpu.CompilerParams(
            dimension_semantics=("arbitrary",), vmem_limit_bytes=VMEM_LIMIT),
        name="merge_mlp",
    )(x2d, ya, yb, yc, n1w, wg, gb, wba, wbb, wbc, wo, n2w, w1, w2)


def kernel(x, norm1_w, w_in, gate_b, a_vnorm_w, a_ws, a_bs, b_qnorm_w, b_knorm_w, b_lam, b_subnorm_w,
           c_conv_w, w_br_a, w_br_b, w_br_c, w_out, norm2_w, w_ff1, w_ff2):
    batch, seq, d = x.shape
    depth = w_in.shape[0]
    slopes = 2.0 ** (-8.0 * jnp.arange(1, B_HEADS + 1, dtype=F32) / B_HEADS)
    o_a, o_q, o_k, o_v, o_c, o_g = 0, 1024, 1536, 2048, 2560, 4096
    x2d = x.reshape(batch * seq, d)
    for l in range(depth):
        w = w_in[l].astype(BF16)
        wa = w[:, o_a:o_q]
        wqvt = jnp.concatenate([w[:, o_q:o_k], w[:, o_v:o_c]], axis=1).T
        wk = w[:, o_k:o_v]
        wc = w[:, o_c:o_g]
        wg = w[:, o_g:]
        n1w = norm1_w[l].reshape(1, d)
        qnw = jnp.tile(b_qnorm_w[l].reshape(2 * B_HEAD_DIM), B_HEADS).reshape(B_QK, 1)
        knw = jnp.tile(b_knorm_w[l].reshape(2 * B_HEAD_DIM), B_HEADS).reshape(1, B_QK)
        lambda_init = 0.8 - 0.6 * math.exp(-0.3 * l)

        ya, qt, k, vt, yc = _proj_call(
            x2d, n1w, wa, wk, wqvt, wc, a_vnorm_w[l].reshape(1, A_WIDTH), a_ws[l], a_bs[l].T,
            qnw, knw, c_conv_w[l], seq=seq)
        yb = _attn_call(slopes, qt, k, vt, b_lam[l], b_subnorm_w[l].reshape(1, B_V_DIM),
                        batch=batch, seq=seq, lambda_init=lambda_init)
        x2d = _mlp_call(
            x2d, ya, yb, yc, n1w, wg, gate_b[l], w_br_a[l].astype(BF16), w_br_b[l].astype(BF16),
            w_br_c[l].astype(BF16), w_out[l].astype(BF16), norm2_w[l].reshape(1, d),
            w_ff1[l].astype(BF16), w_ff2[l].astype(BF16))
    return x2d.reshape(batch, seq, d)
```

```python
import functools
import math

import jax
import jax.numpy as jnp
from jax import lax
from jax.experimental import pallas as pl
from jax.experimental.pallas import tpu as pltpu

D_MODEL = 1024
A_WIDTH = 512
A_GROUPS = 4
A_BLOCK = 128
B_HEADS = 4
B_HEAD_DIM = 64
B_V_DIM = 128
B_QK = 512
B_VW = 512
C_WIDTH = 512
CONV_K = 3
N_BRANCH = 3
D_FF = 4 * D_MODEL
CHUNK = 64
EPS = 1e-6
LOG2E = math.log2(math.e)
NEG_BIG = -1e30

TM_PROJ = 512
TQ = 1024
SUB = 256
N_SUB = TQ // SUB
QK_AHEAD = 2
POS_RADIX = 32
SLOPE_PIECES = 3
ONES_ROWS = 16
TM_MLP = 512
FF_CHUNK = 1024
CARRY_ROWS = 8
VMEM_LIMIT = 56 * 1024 * 1024

BF16 = jnp.bfloat16
F32 = jnp.float32


def _dot(a, b):
    return jnp.dot(a, b, preferred_element_type=F32)


def _dot_nt(a, b):
    return lax.dot_general(a, b, (((1,), (1,)), ((), ())), preferred_element_type=F32)


def _rms(x, w):
    return x * lax.rsqrt(jnp.mean(x * x, axis=-1, keepdims=True) + EPS) * w


def _resident(shape):
    zeros = (0,) * len(shape)
    return pl.BlockSpec(shape, lambda *_: zeros, pipeline_mode=pl.Buffered(1))


def _proj_kernel(x_ref, n1w_ref, wa_ref, wk_ref, wqvt_ref, wc_ref, vnw_ref, ws_ref, bst_ref,
                 qnw_ref, knw_ref, cw_ref,
                 ya_ref, qt_ref, k_ref, vt_ref, yc_ref, zc_ref, *, tiles_per_seq):
    i = pl.program_id(0)
    tm = x_ref.shape[0]

    @pl.when(i % tiles_per_seq == 0)
    def _():
        zc_ref[...] = jnp.zeros_like(zc_ref)

    h = _rms(x_ref[...], n1w_ref[...]).astype(BF16)

    pa = _dot(h, wa_ref[...])
    pc = _dot(h, wc_ref[...])

    u = jax.nn.gelu(pa[:, :A_WIDTH])
    v = _rms(jax.nn.gelu(pa[:, A_WIDTH:]), vnw_ref[...]).astype(BF16)
    r = lax.broadcasted_iota(jnp.int32, (A_BLOCK, A_BLOCK), 0)
    c = lax.broadcasted_iota(jnp.int32, (A_BLOCK, A_BLOCK), 1)
    tril = c <= r
    gd = A_WIDTH // A_GROUPS
    for g in range(A_GROUPS):
        ws = jnp.where(tril, ws_ref[g], 0.0).astype(BF16)
        bias = bst_ref[:, g:g + 1]
        for blk in range(tm // A_BLOCK):
            rows = slice(blk * A_BLOCK, (blk + 1) * A_BLOCK)
            cols = slice(g * gd, (g + 1) * gd)
            sv = _dot(ws, v[rows, cols]) + bias
            ya_ref[rows, cols] = (u[rows, cols] * sv).astype(BF16)

    pqv = _dot_nt(wqvt_ref[...], h)

    z = pc[:, C_WIDTH:2 * C_WIDTH] * pc[:, 2 * C_WIDTH:]
    row = lax.broadcasted_iota(jnp.int32, z.shape, 0)
    last1 = zc_ref[CARRY_ROWS - 1:CARRY_ROWS, :]
    last2 = zc_ref[CARRY_ROWS - 2:CARRY_ROWS - 1, :]
    z1 = jnp.where(row == 0, last1, pltpu.roll(z, 1, axis=0))
    z2 = jnp.where(row == 0, last2, jnp.where(row == 1, last1, pltpu.roll(z, 2, axis=0)))
    conv = cw_ref[0:1, :] * z2 + cw_ref[1:2, :] * z1 + cw_ref[2:3, :] * z
    yc_ref[...] = (pc[:, :C_WIDTH] * conv).astype(BF16)
    zc_ref[...] = z[tm - CARRY_ROWS:, :]

    pk = _dot(h, wk_ref[...])

    q3 = pqv[:B_QK].reshape(B_QK // B_HEAD_DIM, B_HEAD_DIM, tm)
    qs = jnp.mean(q3 * q3, axis=1, keepdims=True)
    qn = q3 * lax.rsqrt(qs + EPS) * qnw_ref[...].reshape(B_QK // B_HEAD_DIM, B_HEAD_DIM, 1)
    qt_ref[...] = (qn * (B_HEAD_DIM ** -0.5 * LOG2E)).reshape(B_QK, tm).astype(BF16)
    vt_ref[...] = pqv[B_QK:].astype(BF16)

    gr = lax.broadcasted_iota(jnp.int32, (256, 256), 0) // B_HEAD_DIM
    gc = lax.broadcasted_iota(jnp.int32, (256, 256), 1) // B_HEAD_DIM
    ones_bd = jnp.where(gr == gc, 1.0, 0.0).astype(BF16)
    for j in range(B_QK // 256):
        cols = slice(j * 256, (j + 1) * 256)
        t = pk[:, cols]
        ss = _dot((t * t).astype(BF16), ones_bd) * (1.0 / B_HEAD_DIM)
        k_ref[:, cols] = (t * lax.rsqrt(ss + EPS) * knw_ref[:, cols]).astype(BF16)


def _proj_call(x2d, n1w, wa, wk, wqvt, wc, vnw, ws, bst, qnw, knw, cw, *, seq):
    m = x2d.shape[0]
    tm = TM_PROJ
    row_bf = lambda width: pl.BlockSpec((tm, width), lambda i: (i, 0))
    col_bf = lambda height: pl.BlockSpec((height, tm), lambda i: (0, i))
    return pl.pallas_call(
        functools.partial(_proj_kernel, tiles_per_seq=seq // tm),
        grid=(m // tm,),
        in_specs=[
            pl.BlockSpec((tm, D_MODEL), lambda i: (i, 0)),
            _resident(n1w.shape), _resident(wa.shape), _resident(wk.shape), _resident(wqvt.shape),
            _resident(wc.shape), _resident(vnw.shape), _resident(ws.shape), _resident(bst.shape),
            _resident(qnw.shape), _resident(knw.shape), _resident(cw.shape),
        ],
        out_specs=[row_bf(A_WIDTH), col_bf(B_QK), row_bf(B_QK), col_bf(B_VW), row_bf(C_WIDTH)],
        out_shape=[
            jax.ShapeDtypeStruct((m, A_WIDTH), BF16),
            jax.ShapeDtypeStruct((B_QK, m), BF16),
            jax.ShapeDtypeStruct((m, B_QK), BF16),
            jax.ShapeDtypeStruct((B_VW, m), BF16),
            jax.ShapeDtypeStruct((m, C_WIDTH), BF16),
        ],
        scratch_shapes=[pltpu.VMEM((CARRY_ROWS, C_WIDTH), F32)],
        compiler_params=pltpu.CompilerParams(
            dimension_semantics=("arbitrary",), vmem_limit_bytes=VMEM_LIMIT),
        name="in_proj",
    )(x2d, n1w, wa, wk, wqvt, wc, vnw, ws, bst, qnw, knw, cw)


def _attn_kernel(slopes_ref, qt_ref, k_ref, vt_ref, lam_ref, snw_ref, o_ref,
                 kfeat_ref, diag_ref, m_ref, l_ref, acc_ref, s_ref, *, lambda_init):
    hd = pl.program_id(1)
    i = pl.program_id(2)
    slope = slopes_ref[hd] * LOG2E

    @pl.when(i == 0)
    def _():
        r = lax.broadcasted_iota(jnp.int32, (SUB, SUB), 0)
        c = lax.broadcasted_iota(jnp.int32, (SUB, SUB), 1)
        visible = (r // CHUNK) <= (c // CHUNK)
        diag_ref[...] = jnp.where(visible, jnp.minimum(2.0 * slope * (c - r).astype(F32), 0.0), NEG_BIG)
        kr = lax.broadcasted_iota(jnp.int32, kfeat_ref.shape, 0)
        kl = lax.broadcasted_iota(jnp.int32, kfeat_ref.shape, 1)
        feat = jnp.where(kl < SLOPE_PIECES, kr // POS_RADIX, kr % POS_RADIX)
        kfeat_ref[...] = jnp.where(kl < 2 * SLOPE_PIECES, feat, 0).astype(F32).astype(BF16)

    pieces = []
    rest = jnp.full((2 * B_HEAD_DIM, SUB), slope, F32)
    for _ in range(SLOPE_PIECES):
        piece = rest.astype(BF16).astype(F32)
        pieces.append(piece)
        rest = rest - piece
    frow = lax.broadcasted_iota(jnp.int32, (2 * B_HEAD_DIM, SUB), 0)
    piece = functools.reduce(lambda acc, n: jnp.where(frow % SLOPE_PIECES == n, pieces[n], acc),
                             range(1, SLOPE_PIECES), pieces[0])
    q_feat = jnp.where(frow < SLOPE_PIECES, float(POS_RADIX) * piece,
                       jnp.where(frow < 2 * SLOPE_PIECES, piece, 0.0)).astype(BF16)

    qt = qt_ref[...]
    first = lax.broadcasted_iota(jnp.int32, qt.shape, 0) < B_HEAD_DIM
    zero = jnp.zeros_like(qt)
    q_maps = (jnp.where(first, qt, zero), jnp.where(first, zero, qt))

    def scores(start, ch, nk):
        a, mp = divmod(ch, 2)
        rows = nk * SUB
        kk = jnp.concatenate([k_ref[pl.ds(start, rows), :], kfeat_ref[0:rows, :]], axis=1)
        qq = jnp.concatenate([q_maps[mp][:, a * SUB:(a + 1) * SUB], q_feat], axis=0)
        return _dot(kk, qq)

    def super_block(key_start, shift, diagonal, next_start):
        def n_key_blocks(a):
            return a + 1 if diagonal else N_SUB

        def softmax_pv(ch, s, old):
            a = ch // 2
            nk = n_key_blocks(a)
            ts = [s[ks * SUB:(ks + 1) * SUB] for ks in range(nk)]
            if diagonal:
                ts[a] = ts[a] + diag_ref[...]
            mb = functools.reduce(jnp.maximum, [jnp.max(t, axis=0, keepdims=True) for t in ts])
            if diagonal:
                m_new = mb
            else:
                m_old, l_old, acc_old = old
                m_new = jnp.maximum(m_old, mb + shift)
                alpha = jnp.exp2(m_old - m_new)
            mm = m_new - shift
            ps = [jnp.exp2(t - mm) for t in ts]
            pb = jnp.concatenate([p.astype(BF16) for p in ps], axis=0)
            vv = jnp.concatenate([vt_ref[:, pl.ds(key_start, nk * SUB)],
                                  jnp.ones((ONES_ROWS, nk * SUB), BF16)], axis=0)
            pv_l = _dot(vv, pb)
            pv, lsum = pv_l[:B_V_DIM], pv_l[B_V_DIM:B_V_DIM + 1]
            if diagonal:
                return m_new, lsum, pv
            return m_new, alpha * l_old + lsum, alpha * acc_old + pv

        n_chains = 2 * N_SUB
        if diagonal:
            pending = [scores(key_start, ch, n_key_blocks(ch // 2)) for ch in range(QK_AHEAD)]
            olds = [None] * n_chains
        else:
            pending = [s_ref[ch] for ch in range(QK_AHEAD)]
            olds = [(m_ref[ch], l_ref[ch], acc_ref[ch]) for ch in range(n_chains)]
        news, prefetched = [], []
        for ch in range(n_chains):
            ahead = ch + QK_AHEAD
            if ahead < n_chains:
                pending.append(scores(key_start, ahead, n_key_blocks(ahead // 2)))
            else:
                prefetched.append(scores(next_start, ahead - n_chains, N_SUB))
            news.append(softmax_pv(ch, pending.pop(0), olds[ch]))
        for ch, (m_new, l_new, acc_new) in enumerate(news):
            m_ref[ch] = m_new
            l_ref[ch] = l_new
            acc_ref[ch] = acc_new
        for n, s in enumerate(prefetched):
            s_ref[n] = s

    super_block(pl.multiple_of(i * TQ, TQ), 0.0, True, 0)

    def off_diag(j, carry):
        shift = (slope * float(TQ)) * (j - i).astype(F32)
        nxt = jnp.minimum(j + 1, i - 1)
        super_block(pl.multiple_of(j * TQ, TQ), shift, False, pl.multiple_of(nxt * TQ, TQ))
        return carry

    lax.fori_loop(0, i, off_diag, 0)

    lp = lam_ref[...]
    lam = (jnp.exp(jnp.sum(lp[0:1] * lp[1:2], axis=-1, keepdims=True))
           - jnp.exp(jnp.sum(lp[2:3] * lp[3:4], axis=-1, keepdims=True)) + lambda_init)
    er = lax.broadcasted_iota(jnp.int32, (SUB, SUB), 0)
    ec = lax.broadcasted_iota(jnp.int32, (SUB, SUB), 1)
    eye = jnp.where(er == ec, 1.0, 0.0).astype(BF16)
    for a in range(N_SUB):
        ot = acc_ref[2 * a] / l_ref[2 * a] - lam * (acc_ref[2 * a + 1] / l_ref[2 * a + 1])
        inv = lax.rsqrt(jnp.mean(ot * ot, axis=0, keepdims=True) + EPS)
        on = (ot * inv * snw_ref[...] * (1.0 - lambda_init)).astype(BF16)
        o_ref[a * SUB:(a + 1) * SUB, :] = _dot_nt(eye, on).astype(BF16)


def _attn_call(slopes, qt, k, vt, lam_p, snw, *, batch, seq, lambda_init):
    m = k.shape[0]
    nq = seq // TQ
    return pl.pallas_call(
        functools.partial(_attn_kernel, lambda_init=lambda_init),
        grid=(batch, B_HEADS, nq),
        in_specs=[
            pl.BlockSpec(memory_space=pltpu.SMEM),
            pl.BlockSpec((B_V_DIM, TQ), lambda b, h, i: (h, b * nq + i)),
            pl.BlockSpec((seq, B_V_DIM), lambda b, h, i: (b, h)),
            pl.BlockSpec((B_V_DIM, seq), lambda b, h, i: (h, b)),
            pl.BlockSpec(lam_p.shape, lambda b, h, i: (0, 0)),
            pl.BlockSpec(snw.shape, lambda b, h, i: (0, 0)),
        ],
        out_specs=pl.BlockSpec((TQ, B_V_DIM), lambda b, h, i: (b * nq + i, h)),
        out_shape=jax.ShapeDtypeStruct((m, B_VW), BF16),
        scratch_shapes=[
            pltpu.VMEM((TQ, 2 * B_HEAD_DIM), BF16),
            pltpu.VMEM((SUB, SUB), F32),
            pltpu.VMEM((2 * N_SUB, 1, SUB), F32),
            pltpu.VMEM((2 * N_SUB, 1, SUB), F32),
            pltpu.VMEM((2 * N_SUB, B_V_DIM, SUB), F32),
            pltpu.VMEM((QK_AHEAD, TQ, SUB), F32),
        ],
        compiler_params=pltpu.CompilerParams(
            dimension_semantics=("arbitrary", "arbitrary", "arbitrary"), vmem_limit_bytes=VMEM_LIMIT),
        name="diff_attn",
    )(slopes, qt, k, vt, lam_p, snw)


def _mlp_kernel(x_ref, ya_ref, yb_ref, yc_ref, n1w_ref, wg_ref, gb_ref, wba_ref, wbb_ref, wbc_ref,
                wo_ref, n2w_ref, w1_ref, w2_ref, o_ref):
    x = x_ref[...]
    h = _rms(x, n1w_ref[...]).astype(BF16)
    merged = None
    for br, (y_ref, w_ref) in enumerate(((ya_ref, wba_ref), (yb_ref, wbb_ref), (yc_ref, wbc_ref))):
        cols = slice(br * D_MODEL, (br + 1) * D_MODEL)
        gate = jax.nn.sigmoid(_dot(h, wg_ref[:, cols]) + gb_ref[br:br + 1, :])
        term = gate * _dot(y_ref[...], w_ref[...])
        merged = term if merged is None else merged + term
    x1 = x + _dot(merged.astype(BF16), wo_ref[...])
    h2 = _rms(x1, n2w_ref[...]).astype(BF16)
    acc = x1
    for cidx in range(D_FF // FF_CHUNK):
        cols = slice(cidx * FF_CHUNK, (cidx + 1) * FF_CHUNK)
        f = jnp.square(jnp.maximum(_dot(h2, w1_ref[:, cols]), 0.0)).astype(BF16)
        acc = acc + _dot(f, w2_ref[cols, :])
    o_ref[...] = acc


def _mlp_call(x2d, ya, yb, yc, n1w, wg, gb, wba, wbb, wbc, wo, n2w, w1, w2):
    m = x2d.shape[0]
    tm = TM_MLP
    rows = lambda width: pl.BlockSpec((tm, width), lambda i: (i, 0))
    return pl.pallas_call(
        _mlp_kernel,
        grid=(m // tm,),
        in_specs=[rows(D_MODEL), rows(A_WIDTH), rows(B_VW), rows(C_WIDTH),
                  _resident(n1w.shape), _resident(wg.shape), _resident(gb.shape),
                  _resident(wba.shape), _resident(wbb.shape), _resident(wbc.shape),
                  _resident(wo.shape), _resident(n2w.shape), _resident(w1.shape), _resident(w2.shape)],
        out_specs=rows(D_MODEL),
        out_shape=jax.ShapeDtypeStruct((m, D_MODEL), F32),
        compiler_params=pltpu.CompilerParams(
            dimension_semantics=("arbitrary",), vmem_limit_bytes=VMEM_LIMIT),
        name="merge_mlp",
    )(x2d, ya, yb, yc, n1w, wg, gb, wba, wbb, wbc, wo, n2w, w1, w2)


def kernel(x, norm1_w, w_in, gate_b, a_vnorm_w, a_ws, a_bs, b_qnorm_w, b_knorm_w, b_lam, b_subnorm_w,
           c_conv_w, w_br_a, w_br_b, w_br_c, w_out, norm2_w, w_ff1, w_ff2):
    batch, seq, d = x.shape
    depth = w_in.shape[0]
    slopes = 2.0 ** (-8.0 * jnp.arange(1, B_HEADS + 1, dtype=F32) / B_HEADS)
    o_a, o_q, o_k, o_v, o_c, o_g = 0, 1024, 1536, 2048, 2560, 4096
    x2d = x.reshape(batch * seq, d)
    for l in range(depth):
        w = w_in[l].astype(BF16)
        wa = w[:, o_a:o_q]
        wqvt = jnp.concatenate([w[:, o_q:o_k], w[:, o_v:o_c]], axis=1).T
        wk = w[:, o_k:o_v]
        wc = w[:, o_c:o_g]
        wg = w[:, o_g:]
        n1w = norm1_w[l].reshape(1, d)
        qnw = jnp.tile(b_qnorm_w[l].reshape(2 * B_HEAD_DIM), B_HEADS).reshape(B_QK, 1)
        knw = jnp.tile(b_knorm_w[l].reshape(2 * B_HEAD_DIM), B_HEADS).reshape(1, B_QK)
        lambda_init = 0.8 - 0.6 * math.exp(-0.3 * l)

        ya, qt, k, vt, yc = _proj_call(
            x2d, n1w, wa, wk, wqvt, wc, a_vnorm_w[l].reshape(1, A_WIDTH), a_ws[l], a_bs[l].T,
            qnw, knw, c_conv_w[l], seq=seq)
        yb = _attn_call(slopes, qt, k, vt, b_lam[l], b_subnorm_w[l].reshape(B_V_DIM, 1),
                        batch=batch, seq=seq, lambda_init=lambda_init)
        x2d = _mlp_call(
            x2d, ya, yb, yc, n1w, wg, gate_b[l], w_br_a[l].astype(BF16), w_br_b[l].astype(BF16),
            w_br_c[l].astype(BF16), w_out[l].astype(BF16), norm2_w[l].reshape(1, d),
            w_ff1[l].astype(BF16), w_ff2[l].astype(BF16))
    return x2d.reshape(batch, seq, d)
```

```python
import functools
import math

import jax
import jax.numpy as jnp
from jax import lax
from jax.experimental import pallas as pl
from jax.experimental.pallas import tpu as pltpu

D_MODEL = 1024
A_WIDTH = 512
A_GROUPS = 4
A_BLOCK = 128
B_HEADS = 4
B_HEAD_DIM = 64
B_V_DIM = 128
B_QK = 512
B_VW = 512
C_WIDTH = 512
CONV_K = 3
N_BRANCH = 3
D_FF = 4 * D_MODEL
CHUNK = 64
EPS = 1e-6
LOG2E = math.log2(math.e)
NEG_BIG = -1e30

TM_PROJ = 1024
TQ = 1024
SUB = 256
N_SUB = TQ // SUB
QK_AHEAD = 2
POS_RADIX = 32
SLOPE_PIECES = 3
ONES_ROWS = 16
TM_MLP = 512
FF_CHUNK = 1024
CARRY_ROWS = 8
WT_CHUNK = 256
W_A, W_Q, W_K, W_V, W_C, W_G = 0, 1024, 1536, 2048, 2560, 4096
VMEM_LIMIT = 56 * 1024 * 1024

BF16 = jnp.bfloat16
F32 = jnp.float32


def _dot(a, b):
    return jnp.dot(a, b, preferred_element_type=F32)


def _dot_nt(a, b):
    return lax.dot_general(a, b, (((1,), (1,)), ((), ())), preferred_element_type=F32)


def _rms(x, w):
    return x * lax.rsqrt(jnp.mean(x * x, axis=-1, keepdims=True) + EPS) * w


def _proj_kernel(x_ref, n1w_ref, wa_ref, wq_ref, wk_ref, wv_ref, wcb_ref, wcc_ref, wcx_ref,
                 vnw_ref, ws_ref, bst_ref, qnw_ref, knw_ref, cw_ref,
                 ya_ref, qt_ref, k_ref, vt_ref, yc_ref, zc_ref, wqvt_ref, *, tiles_per_seq):
    i = pl.program_id(0)
    tm = x_ref.shape[0]

    @pl.when(i % tiles_per_seq == 0)
    def _():
        zc_ref[...] = jnp.zeros_like(zc_ref)

    @pl.when(i == 0)
    def _():
        for n, w_ref in enumerate((wq_ref, wv_ref)):
            for cb in range(B_QK // WT_CHUNK):
                cols = slice(cb * WT_CHUNK, (cb + 1) * WT_CHUNK)
                wqvt_ref[n * B_QK + cb * WT_CHUNK:n * B_QK + (cb + 1) * WT_CHUNK, :] = (
                    w_ref[:, cols].astype(F32).T.astype(BF16))

    h = _rms(x_ref[...], n1w_ref[...]).astype(BF16)

    pa = _dot(h, wa_ref[...])
    pcb = _dot(h, wcb_ref[...])
    z = _dot(h, wcc_ref[...]) * _dot(h, wcx_ref[...])

    u = jax.nn.gelu(pa[:, :A_WIDTH])
    v = _rms(jax.nn.gelu(pa[:, A_WIDTH:]), vnw_ref[...]).astype(BF16)
    r = lax.broadcasted_iota(jnp.int32, (A_BLOCK, A_BLOCK), 0)
    c = lax.broadcasted_iota(jnp.int32, (A_BLOCK, A_BLOCK), 1)
    tril = c <= r
    gd = A_WIDTH // A_GROUPS
    for g in range(A_GROUPS):
        ws = jnp.where(tril, ws_ref[g], 0.0).astype(BF16)
        bias = bst_ref[:, g:g + 1]
        for blk in range(tm // A_BLOCK):
            rows = slice(blk * A_BLOCK, (blk + 1) * A_BLOCK)
            cols = slice(g * gd, (g + 1) * gd)
            sv = _dot(ws, v[rows, cols]) + bias
            ya_ref[rows, cols] = (u[rows, cols] * sv).astype(BF16)

    pqv = _dot_nt(wqvt_ref[...], h)

    row = lax.broadcasted_iota(jnp.int32, z.shape, 0)
    last1 = zc_ref[CARRY_ROWS - 1:CARRY_ROWS, :]
    last2 = zc_ref[CARRY_ROWS - 2:CARRY_ROWS - 1, :]
    z1 = jnp.where(row == 0, last1, pltpu.roll(z, 1, axis=0))
    z2 = jnp.where(row == 0, last2, jnp.where(row == 1, last1, pltpu.roll(z, 2, axis=0)))
    conv = cw_ref[0:1, :] * z2 + cw_ref[1:2, :] * z1 + cw_ref[2:3, :] * z
    yc_ref[...] = (pcb * conv).astype(BF16)
    zc_ref[...] = z[tm - CARRY_ROWS:, :]

    pk = _dot(h, wk_ref[...])

    q3 = pqv[:B_QK].reshape(B_QK // B_HEAD_DIM, B_HEAD_DIM, tm)
    qs = jnp.mean(q3 * q3, axis=1, keepdims=True)
    qn = q3 * lax.rsqrt(qs + EPS) * qnw_ref[...].reshape(B_QK // B_HEAD_DIM, B_HEAD_DIM, 1)
    qt_ref[...] = (qn * (B_HEAD_DIM ** -0.5 * LOG2E)).reshape(B_QK, tm).astype(BF16)
    vt_ref[...] = pqv[B_QK:].astype(BF16)

    gr = lax.broadcasted_iota(jnp.int32, (256, 256), 0) // B_HEAD_DIM
    gc = lax.broadcasted_iota(jnp.int32, (256, 256), 1) // B_HEAD_DIM
    ones_bd = jnp.where(gr == gc, 1.0, 0.0).astype(BF16)
    for j in range(B_QK // 256):
        cols = slice(j * 256, (j + 1) * 256)
        t = pk[:, cols]
        ss = _dot((t * t).astype(BF16), ones_bd) * (1.0 / B_HEAD_DIM)
        k_ref[:, cols] = (t * lax.rsqrt(ss + EPS) * knw_ref[:, cols]).astype(BF16)


def _layer_block(l, shape, col=0):
    nd = len(shape)
    return pl.BlockSpec((None,) + tuple(shape), lambda *_: (l,) + (0,) * (nd - 1) + (col,),
                        pipeline_mode=pl.Buffered(1))


def _proj_call(l, x2d, n1w, w_in, vnw, ws, bst, qnw, knw, cw, *, seq):
    m = x2d.shape[0]
    tm = TM_PROJ
    row_bf = lambda width: pl.BlockSpec((tm, width), lambda i: (i, 0))
    col_bf = lambda height: pl.BlockSpec((height, tm), lambda i: (0, i))
    w_cols = lambda width, start: _layer_block(l, (D_MODEL, width), start // width)
    return pl.pallas_call(
        functools.partial(_proj_kernel, tiles_per_seq=seq // tm),
        grid=(m // tm,),
        in_specs=[
            pl.BlockSpec((tm, D_MODEL), lambda i: (i, 0)),
            _layer_block(l, n1w.shape[1:]),
            w_cols(2 * A_WIDTH, W_A), w_cols(B_QK, W_Q), w_cols(B_QK, W_K), w_cols(B_VW, W_V),
            w_cols(C_WIDTH, W_C), w_cols(C_WIDTH, W_C + C_WIDTH), w_cols(C_WIDTH, W_C + 2 * C_WIDTH),
            _layer_block(l, vnw.shape[1:]), _layer_block(l, ws.shape[1:]), _layer_block(l, bst.shape[1:]),
            _layer_block(l, qnw.shape[1:]), _layer_block(l, knw.shape[1:]), _layer_block(l, cw.shape[1:]),
        ],
        out_specs=[row_bf(A_WIDTH), col_bf(B_QK), row_bf(B_QK), col_bf(B_VW), row_bf(C_WIDTH)],
        out_shape=[
            jax.ShapeDtypeStruct((m, A_WIDTH), BF16),
            jax.ShapeDtypeStruct((B_QK, m), BF16),
            jax.ShapeDtypeStruct((m, B_QK), BF16),
            jax.ShapeDtypeStruct((B_VW, m), BF16),
            jax.ShapeDtypeStruct((m, C_WIDTH), BF16),
        ],
        scratch_shapes=[pltpu.VMEM((CARRY_ROWS, C_WIDTH), F32),
                        pltpu.VMEM((B_QK + B_VW, D_MODEL), BF16)],
        compiler_params=pltpu.CompilerParams(
            dimension_semantics=("arbitrary",), vmem_limit_bytes=VMEM_LIMIT),
        name="in_proj",
    )(x2d, n1w, w_in, w_in, w_in, w_in, w_in, w_in, w_in, vnw, ws, bst, qnw, knw, cw)


def _attn_kernel(slopes_ref, qt_ref, k_ref, vt_ref, lam_ref, snw_ref, o_ref,
                 kfeat_ref, diag_ref, m_ref, l_ref, acc_ref, s_ref, *, lambda_init):
    hd = pl.program_id(1)
    i = pl.program_id(2)
    slope = slopes_ref[hd] * LOG2E

    @pl.when(i == 0)
    def _():
        r = lax.broadcasted_iota(jnp.int32, (SUB, SUB), 0)
        c = lax.broadcasted_iota(jnp.int32, (SUB, SUB), 1)
        visible = (r // CHUNK) <= (c // CHUNK)
        diag_ref[...] = jnp.where(visible, jnp.minimum(2.0 * slope * (c - r).astype(F32), 0.0), NEG_BIG)
        kr = lax.broadcasted_iota(jnp.int32, kfeat_ref.shape, 0)
        kl = lax.broadcasted_iota(jnp.int32, kfeat_ref.shape, 1)
        feat = jnp.where(kl < SLOPE_PIECES, kr // POS_RADIX, kr % POS_RADIX)
        kfeat_ref[...] = jnp.where(kl < 2 * SLOPE_PIECES, feat, 0).astype(F32).astype(BF16)

    pieces = []
    rest = jnp.full((2 * B_HEAD_DIM, SUB), slope, F32)
    for _ in range(SLOPE_PIECES):
        piece = rest.astype(BF16).astype(F32)
        pieces.append(piece)
        rest = rest - piece
    frow = lax.broadcasted_iota(jnp.int32, (2 * B_HEAD_DIM, SUB), 0)
    piece = functools.reduce(lambda acc, n: jnp.where(frow % SLOPE_PIECES == n, pieces[n], acc),
                             range(1, SLOPE_PIECES), pieces[0])
    q_feat = jnp.where(frow < SLOPE_PIECES, float(POS_RADIX) * piece,
                       jnp.where(frow < 2 * SLOPE_PIECES, piece, 0.0)).astype(BF16)

    qt = qt_ref[...]
    first = lax.broadcasted_iota(jnp.int32, qt.shape, 0) < B_HEAD_DIM
    zero = jnp.zeros_like(qt)
    q_maps = (jnp.where(first, qt, zero), jnp.where(first, zero, qt))

    def scores(start, ch, nk):
        a, mp = divmod(ch, 2)
        rows = nk * SUB
        kk = jnp.concatenate([k_ref[pl.ds(start, rows), :], kfeat_ref[0:rows, :]], axis=1)
        qq = jnp.concatenate([q_maps[mp][:, a * SUB:(a + 1) * SUB], q_feat], axis=0)
        return _dot(kk, qq)

    def super_block(key_start, shift, diagonal, next_start):
        def n_key_blocks(a):
            return a + 1 if diagonal else N_SUB

        def softmax_pv(ch, s, old):
            a = ch // 2
            nk = n_key_blocks(a)
            ts = [s[ks * SUB:(ks + 1) * SUB] for ks in range(nk)]
            if diagonal:
                ts[a] = ts[a] + diag_ref[...]
            mb = functools.reduce(jnp.maximum, [jnp.max(t, axis=0, keepdims=True) for t in ts])
            if diagonal:
                m_new = mb
            else:
                m_old, l_old, acc_old = old
                m_new = jnp.maximum(m_old, mb + shift)
                alpha = jnp.exp2(m_old - m_new)
            mm = m_new - shift
            ps = [jnp.exp2(t - mm) for t in ts]
            pb = jnp.concatenate([p.astype(BF16) for p in ps], axis=0)
            vv = jnp.concatenate([vt_ref[:, pl.ds(key_start, nk * SUB)],
                                  jnp.ones((ONES_ROWS, nk * SUB), BF16)], axis=0)
            pv_l = _dot(vv, pb)
            pv, lsum = pv_l[:B_V_DIM], pv_l[B_V_DIM:B_V_DIM + 1]
            if diagonal:
                return m_new, lsum, pv
            return m_new, alpha * l_old + lsum, alpha * acc_old + pv

        n_chains = 2 * N_SUB
        if diagonal:
            pending = [scores(key_start, ch, n_key_blocks(ch // 2)) for ch in range(QK_AHEAD)]
            olds = [None] * n_chains
        else:
            pending = [s_ref[ch] for ch in range(QK_AHEAD)]
            olds = [(m_ref[ch], l_ref[ch], acc_ref[ch]) for ch in range(n_chains)]
        news, prefetched = [], []
        for ch in range(n_chains):
            ahead = ch + QK_AHEAD
            if ahead < n_chains:
                pending.append(scores(key_start, ahead, n_key_blocks(ahead // 2)))
            else:
                prefetched.append(scores(next_start, ahead - n_chains, N_SUB))
            news.append(softmax_pv(ch, pending.pop(0), olds[ch]))
        for ch, (m_new, l_new, acc_new) in enumerate(news):
            m_ref[ch] = m_new
            l_ref[ch] = l_new
            acc_ref[ch] = acc_new
        for n, s in enumerate(prefetched):
            s_ref[n] = s

    super_block(pl.multiple_of(i * TQ, TQ), 0.0, True, 0)

    def off_diag(j, carry):
        shift = (slope * float(TQ)) * (j - i).astype(F32)
        nxt = jnp.minimum(j + 1, i - 1)
        super_block(pl.multiple_of(j * TQ, TQ), shift, False, pl.multiple_of(nxt * TQ, TQ))
        return carry

    lax.fori_loop(0, i, off_diag, 0)

    lp = lam_ref[...]
    lam = (jnp.exp(jnp.sum(lp[0:1] * lp[1:2], axis=-1, keepdims=True))
           - jnp.exp(jnp.sum(lp[2:3] * lp[3:4], axis=-1, keepdims=True)) + lambda_init)
    er = lax.broadcasted_iota(jnp.int32, (SUB, SUB), 0)
    ec = lax.broadcasted_iota(jnp.int32, (SUB, SUB), 1)
    eye = jnp.where(er == ec, 1.0, 0.0).astype(BF16)
    for a in range(N_SUB):
        ot = acc_ref[2 * a] / l_ref[2 * a] - lam * (acc_ref[2 * a + 1] / l_ref[2 * a + 1])
        inv = lax.rsqrt(jnp.mean(ot * ot, axis=0, keepdims=True) + EPS)
        on = (ot * inv * snw_ref[...] * (1.0 - lambda_init)).astype(BF16)
        o_ref[a * SUB:(a + 1) * SUB, :] = _dot_nt(eye, on).astype(BF16)


def _attn_call(l, slopes, qt, k, vt, lam_p, snw, *, batch, seq, lambda_init):
    m = k.shape[0]
    nq = seq // TQ
    return pl.pallas_call(
        functools.partial(_attn_kernel, lambda_init=lambda_init),
        grid=(batch, B_HEADS, nq),
        in_specs=[
            pl.BlockSpec(memory_space=pltpu.SMEM),
            pl.BlockSpec((B_V_DIM, TQ), lambda b, h, i: (h, b * nq + i)),
            pl.BlockSpec((seq, B_V_DIM), lambda b, h, i: (b, h)),
            pl.BlockSpec((B_V_DIM, seq), lambda b, h, i: (h, b)),
            pl.BlockSpec((None,) + lam_p.shape[1:], lambda b, h, i: (l, 0, 0)),
            pl.BlockSpec((None,) + snw.shape[1:], lambda b, h, i: (l, 0, 0)),
        ],
        out_specs=pl.BlockSpec((TQ, B_V_DIM), lambda b, h, i: (b * nq + i, h)),
        out_shape=jax.ShapeDtypeStruct((m, B_VW), BF16),
        scratch_shapes=[
            pltpu.VMEM((TQ, 2 * B_HEAD_DIM), BF16),
            pltpu.VMEM((SUB, SUB), F32),
            pltpu.VMEM((2 * N_SUB, 1, SUB), F32),
            pltpu.VMEM((2 * N_SUB, 1, SUB), F32),
            pltpu.VMEM((2 * N_SUB, B_V_DIM, SUB), F32),
            pltpu.VMEM((QK_AHEAD, TQ, SUB), F32),
        ],
        compiler_params=pltpu.CompilerParams(
            dimension_semantics=("arbitrary", "arbitrary", "arbitrary"), vmem_limit_bytes=VMEM_LIMIT),
        name="diff_attn",
    )(slopes, qt, k, vt, lam_p, snw)


def _mlp_kernel(x_ref, ya_ref, yb_ref, yc_ref, n1w_ref, wg0_ref, wg1_ref, wg2_ref, gb_ref,
                wba_ref, wbb_ref, wbc_ref, wo_ref, n2w_ref, w1_ref, w2_ref, o_ref):
    x = x_ref[...]
    h = _rms(x, n1w_ref[...]).astype(BF16)
    merged = None
    branches = ((ya_ref, wba_ref, wg0_ref), (yb_ref, wbb_ref, wg1_ref), (yc_ref, wbc_ref, wg2_ref))
    for br, (y_ref, w_ref, wg_ref) in enumerate(branches):
        gate = jax.nn.sigmoid(_dot(h, wg_ref[...]) + gb_ref[br:br + 1, :])
        term = gate * _dot(y_ref[...], w_ref[...])
        merged = term if merged is None else merged + term
    x1 = x + _dot(merged.astype(BF16), wo_ref[...])
    h2 = _rms(x1, n2w_ref[...]).astype(BF16)
    acc = x1
    for cidx in range(D_FF // FF_CHUNK):
        cols = slice(cidx * FF_CHUNK, (cidx + 1) * FF_CHUNK)
        f = jnp.square(jnp.maximum(_dot(h2, w1_ref[:, cols]), 0.0)).astype(BF16)
        acc = acc + _dot(f, w2_ref[cols, :])
    o_ref[...] = acc


def _mlp_call(l, x2d, ya, yb, yc, n1w, w_in, gb, wba, wbb, wbc, wo, n2w, w1, w2):
    m = x2d.shape[0]
    tm = TM_MLP
    rows = lambda width: pl.BlockSpec((tm, width), lambda i: (i, 0))
    whole = lambda arr: _layer_block(l, arr.shape[1:])
    gate_cols = lambda br: _layer_block(l, (D_MODEL, D_MODEL), W_G // D_MODEL + br)
    return pl.pallas_call(
        _mlp_kernel,
        grid=(m // tm,),
        in_specs=[rows(D_MODEL), rows(A_WIDTH), rows(B_VW), rows(C_WIDTH),
                  whole(n1w), gate_cols(0), gate_cols(1), gate_cols(2), whole(gb),
                  whole(wba), whole(wbb), whole(wbc), whole(wo), whole(n2w), whole(w1), whole(w2)],
        out_specs=rows(D_MODEL),
        out_shape=jax.ShapeDtypeStruct((m, D_MODEL), F32),
        compiler_params=pltpu.CompilerParams(
            dimension_semantics=("arbitrary",), vmem_limit_bytes=VMEM_LIMIT),
        name="merge_mlp",
    )(x2d, ya, yb, yc, n1w, w_in, w_in, w_in, gb, wba, wbb, wbc, wo, n2w, w1, w2)


def kernel(x, norm1_w, w_in, gate_b, a_vnorm_w, a_ws, a_bs, b_qnorm_w, b_knorm_w, b_lam, b_subnorm_w,
           c_conv_w, w_br_a, w_br_b, w_br_c, w_out, norm2_w, w_ff1, w_ff2):
    batch, seq, d = x.shape
    depth = w_in.shape[0]
    slopes = 2.0 ** (-8.0 * jnp.arange(1, B_HEADS + 1, dtype=F32) / B_HEADS)
    w_in_b, w_ff1_b, w_ff2_b, w_out_b = (w.astype(BF16) for w in (w_in, w_ff1, w_ff2, w_out))
    w_br_a_b, w_br_b_b, w_br_c_b = (w.astype(BF16) for w in (w_br_a, w_br_b, w_br_c))
    n1w = norm1_w.reshape(depth, 1, d)
    n2w = norm2_w.reshape(depth, 1, d)
    vnw = a_vnorm_w.reshape(depth, 1, A_WIDTH)
    bst = a_bs.transpose(0, 2, 1)
    qnw = jnp.tile(b_qnorm_w.reshape(depth, 1, 2 * B_HEAD_DIM), (1, B_HEADS, 1)).reshape(depth, B_QK, 1)
    knw = jnp.tile(b_knorm_w.reshape(depth, 1, 2 * B_HEAD_DIM), (1, B_HEADS, 1)).reshape(depth, 1, B_QK)
    snw = b_subnorm_w.reshape(depth, B_V_DIM, 1)
    x2d = x.reshape(batch * seq, d)
    for l in range(depth):
        lambda_init = 0.8 - 0.6 * math.exp(-0.3 * l)
        ya, qt, k, vt, yc = _proj_call(l, x2d, n1w, w_in_b, vnw, a_ws, bst, qnw, knw, c_conv_w, seq=seq)
        yb = _attn_call(l, slopes, qt, k, vt, b_lam, snw, batch=batch, seq=seq, lambda_init=lambda_init)
        x2d = _mlp_call(l, x2d, ya, yb, yc, n1w, w_in_b, gate_b, w_br_a_b, w_br_b_b, w_br_c_b, w_out_b,
                        n2w, w_ff1_b, w_ff2_b)
    return x2d.reshape(batch, seq, d)
```

```python
import functools
import math

import jax
import jax.numpy as jnp
from jax import lax
from jax.experimental import pallas as pl
from jax.experimental.pallas import tpu as pltpu

D_MODEL = 1024
A_WIDTH = 512
A_GROUPS = 4
A_BLOCK = 128
B_HEADS = 4
B_HEAD_DIM = 64
B_V_DIM = 128
B_QK = 512
B_VW = 512
C_WIDTH = 512
CONV_K = 3
N_BRANCH = 3
D_FF = 4 * D_MODEL
CHUNK = 64
EPS = 1e-6
LOG2E = math.log2(math.e)
NEG_BIG = -1e30

TM_PROJ = 1024
TQ = 1024
SUB = 256
N_SUB = TQ // SUB
QK_AHEAD = 2
POS_RADIX = 32
SLOPE_PIECES = 3
ONES_ROWS = 16
TM_MLP = 512
FF_CHUNK = 1024
CARRY_ROWS = 8
WT_CHUNK = 256
W_A, W_Q, W_K, W_V, W_C, W_G = 0, 1024, 1536, 2048, 2560, 4096
VMEM_LIMIT = 56 * 1024 * 1024

BF16 = jnp.bfloat16
F32 = jnp.float32


def _dot(a, b):
    return jnp.dot(a, b, preferred_element_type=F32)


def _dot_nt(a, b):
    return lax.dot_general(a, b, (((1,), (1,)), ((), ())), preferred_element_type=F32)


def _rms(x, w):
    return x * lax.rsqrt(jnp.mean(x * x, axis=-1, keepdims=True) + EPS) * w


def _proj_kernel(x_ref, n1w_ref, wa_ref, wq_ref, wk_ref, wv_ref, wcb_ref, wcc_ref, wcx_ref,
                 vnw_ref, ws_ref, bst_ref, qnw_ref, knw_ref, cw_ref,
                 ya_ref, qt_ref, k_ref, vt_ref, yc_ref, zc_ref, wqvt_ref, *, tiles_per_seq):
    i = pl.program_id(0)
    tm = x_ref.shape[0]

    @pl.when(i % tiles_per_seq == 0)
    def _():
        zc_ref[...] = jnp.zeros_like(zc_ref)

    @pl.when(i == 0)
    def _():
        for n, w_ref in enumerate((wq_ref, wv_ref)):
            for cb in range(B_QK // WT_CHUNK):
                cols = slice(cb * WT_CHUNK, (cb + 1) * WT_CHUNK)
                wqvt_ref[n * B_QK + cb * WT_CHUNK:n * B_QK + (cb + 1) * WT_CHUNK, :] = (
                    w_ref[:, cols].astype(F32).T.astype(BF16))

    h = _rms(x_ref[...], n1w_ref[...]).astype(BF16)

    pa = _dot(h, wa_ref[...])
    pcb = _dot(h, wcb_ref[...])
    z = _dot(h, wcc_ref[...]) * _dot(h, wcx_ref[...])

    u = jax.nn.gelu(pa[:, :A_WIDTH])
    v = _rms(jax.nn.gelu(pa[:, A_WIDTH:]), vnw_ref[...]).astype(BF16)
    r = lax.broadcasted_iota(jnp.int32, (A_BLOCK, A_BLOCK), 0)
    c = lax.broadcasted_iota(jnp.int32, (A_BLOCK, A_BLOCK), 1)
    tril = c <= r
    gd = A_WIDTH // A_GROUPS
    for g in range(A_GROUPS):
        ws = jnp.where(tril, ws_ref[g], 0.0).astype(BF16)
        bias = bst_ref[:, g:g + 1]
        for blk in range(tm // A_BLOCK):
            rows = slice(blk * A_BLOCK, (blk + 1) * A_BLOCK)
            cols = slice(g * gd, (g + 1) * gd)
            sv = _dot(ws, v[rows, cols]) + bias
            ya_ref[rows, cols] = (u[rows, cols] * sv).astype(BF16)

    pqv = _dot_nt(wqvt_ref[...], h)

    row = lax.broadcasted_iota(jnp.int32, z.shape, 0)
    last1 = zc_ref[CARRY_ROWS - 1:CARRY_ROWS, :]
    last2 = zc_ref[CARRY_ROWS - 2:CARRY_ROWS - 1, :]
    z1 = jnp.where(row == 0, last1, pltpu.roll(z, 1, axis=0))
    z2 = jnp.where(row == 0, last2, jnp.where(row == 1, last1, pltpu.roll(z, 2, axis=0)))
    conv = cw_ref[0:1, :] * z2 + cw_ref[1:2, :] * z1 + cw_ref[2:3, :] * z
    yc_ref[...] = (pcb * conv).astype(BF16)
    zc_ref[...] = z[tm - CARRY_ROWS:, :]

    pk = _dot(h, wk_ref[...])

    q3 = pqv[:B_QK].reshape(B_QK // B_HEAD_DIM, B_HEAD_DIM, tm)
    qs = jnp.mean(q3 * q3, axis=1, keepdims=True)
    qn = q3 * lax.rsqrt(qs + EPS) * qnw_ref[...].reshape(B_QK // B_HEAD_DIM, B_HEAD_DIM, 1)
    qt_ref[...] = (qn * (B_HEAD_DIM ** -0.5 * LOG2E)).reshape(B_QK, tm).astype(BF16)
    vt_ref[...] = pqv[B_QK:].astype(BF16)

    gr = lax.broadcasted_iota(jnp.int32, (256, 256), 0) // B_HEAD_DIM
    gc = lax.broadcasted_iota(jnp.int32, (256, 256), 1) // B_HEAD_DIM
    ones_bd = jnp.where(gr == gc, 1.0, 0.0).astype(BF16)
    for j in range(B_QK // 256):
        cols = slice(j * 256, (j + 1) * 256)
        t = pk[:, cols]
        ss = _dot((t * t).astype(BF16), ones_bd) * (1.0 / B_HEAD_DIM)
        k_ref[:, cols] = (t * lax.rsqrt(ss + EPS) * knw_ref[:, cols]).astype(BF16)


def _layer_block(l, shape, col=0):
    nd = len(shape)
    return pl.BlockSpec((None,) + tuple(shape), lambda *_: (l,) + (0,) * (nd - 1) + (col,),
                        pipeline_mode=pl.Buffered(1))


def _proj_call(l, x2d, n1w, w_in, vnw, ws, bst, qnw, knw, cw, *, seq):
    m = x2d.shape[0]
    tm = TM_PROJ
    row_bf = lambda width: pl.BlockSpec((tm, width), lambda i: (i, 0))
    col_bf = lambda height: pl.BlockSpec((height, tm), lambda i: (0, i))
    w_cols = lambda width, start: _layer_block(l, (D_MODEL, width), start // width)
    return pl.pallas_call(
        functools.partial(_proj_kernel, tiles_per_seq=seq // tm),
        grid=(m // tm,),
        in_specs=[
            pl.BlockSpec((tm, D_MODEL), lambda i: (i, 0)),
            _layer_block(l, n1w.shape[1:]),
            w_cols(2 * A_WIDTH, W_A), w_cols(B_QK, W_Q), w_cols(B_QK, W_K), w_cols(B_VW, W_V),
            w_cols(C_WIDTH, W_C), w_cols(C_WIDTH, W_C + C_WIDTH), w_cols(C_WIDTH, W_C + 2 * C_WIDTH),
            _layer_block(l, vnw.shape[1:]), _layer_block(l, ws.shape[1:]), _layer_block(l, bst.shape[1:]),
            _layer_block(l, qnw.shape[1:]), _layer_block(l, knw.shape[1:]), _layer_block(l, cw.shape[1:]),
        ],
        out_specs=[row_bf(A_WIDTH), col_bf(B_QK), row_bf(B_QK), col_bf(B_VW), row_bf(C_WIDTH)],
        out_shape=[
            jax.ShapeDtypeStruct((m, A_WIDTH), BF16),
            jax.ShapeDtypeStruct((B_QK, m), BF16),
            jax.ShapeDtypeStruct((m, B_QK), BF16),
            jax.ShapeDtypeStruct((B_VW, m), BF16),
            jax.ShapeDtypeStruct((m, C_WIDTH), BF16),
        ],
        scratch_shapes=[pltpu.VMEM((CARRY_ROWS, C_WIDTH), F32),
                        pltpu.VMEM((B_QK + B_VW, D_MODEL), BF16)],
        compiler_params=pltpu.CompilerParams(
            dimension_semantics=("arbitrary",), vmem_limit_bytes=VMEM_LIMIT),
        name="in_proj",
    )(x2d, n1w, w_in, w_in, w_in, w_in, w_in, w_in, w_in, vnw, ws, bst, qnw, knw, cw)


def _attn_kernel(slopes_ref, tasks_ref, qt_ref, k_ref, vt_ref, lam_ref, snw_ref, o_ref,
                 kfeat_ref, diag_ref, m_ref, l_ref, acc_ref, s_ref, *, lambda_init, n_tiles, n_tasks):
    hd = pl.program_id(1)
    slope = slopes_ref[hd] * LOG2E
    n_chains = 2 * N_SUB

    r = lax.broadcasted_iota(jnp.int32, (SUB, SUB), 0)
    c = lax.broadcasted_iota(jnp.int32, (SUB, SUB), 1)
    visible = (r // CHUNK) <= (c // CHUNK)
    diag_ref[...] = jnp.where(visible, jnp.minimum(2.0 * slope * (c - r).astype(F32), 0.0), NEG_BIG)
    kr = lax.broadcasted_iota(jnp.int32, kfeat_ref.shape, 0)
    kl = lax.broadcasted_iota(jnp.int32, kfeat_ref.shape, 1)
    feat = jnp.where(kl < SLOPE_PIECES, kr // POS_RADIX, kr % POS_RADIX)
    kfeat_ref[...] = jnp.where(kl < 2 * SLOPE_PIECES, feat, 0).astype(F32).astype(BF16)

    pieces = []
    rest = jnp.full((2 * B_HEAD_DIM, SUB), slope, F32)
    for _ in range(SLOPE_PIECES):
        piece = rest.astype(BF16).astype(F32)
        pieces.append(piece)
        rest = rest - piece
    frow = lax.broadcasted_iota(jnp.int32, (2 * B_HEAD_DIM, SUB), 0)
    piece = functools.reduce(lambda acc, n: jnp.where(frow % SLOPE_PIECES == n, pieces[n], acc),
                             range(1, SLOPE_PIECES), pieces[0])
    q_feat = jnp.where(frow < SLOPE_PIECES, float(POS_RADIX) * piece,
                       jnp.where(frow < 2 * SLOPE_PIECES, piece, 0.0)).astype(BF16)

    first = frow < B_HEAD_DIM

    def scores(q_start, key_start, ch, nk):
        a, mp = divmod(ch, 2)
        rows = nk * SUB
        kk = jnp.concatenate([k_ref[pl.ds(key_start, rows), :], kfeat_ref[0:rows, :]], axis=1)
        q = qt_ref[:, pl.ds(q_start + a * SUB, SUB)]
        q = jnp.where(first, q, jnp.zeros_like(q)) if mp == 0 else jnp.where(first, jnp.zeros_like(q), q)
        return _dot(kk, jnp.concatenate([q, q_feat], axis=0))

    def softmax_pv(s, key_start, nk, shift, old, diag_block):
        ts = [s[ks * SUB:(ks + 1) * SUB] for ks in range(nk)]
        if diag_block is not None:
            ts[diag_block] = ts[diag_block] + diag_ref[...]
        mb = functools.reduce(jnp.maximum, [jnp.max(t, axis=0, keepdims=True) for t in ts])
        if old is None:
            m_new = mb
        else:
            m_old, l_old, acc_old = old
            m_new = jnp.maximum(m_old, mb + shift)
            alpha = jnp.exp2(m_old - m_new)
        mm = m_new - shift
        pb = jnp.concatenate([jnp.exp2(t - mm).astype(BF16) for t in ts], axis=0)
        vv = jnp.concatenate([vt_ref[:, pl.ds(key_start, nk * SUB)],
                              jnp.ones((ONES_ROWS, nk * SUB), BF16)], axis=0)
        pv_l = _dot(vv, pb)
        pv, lsum = pv_l[:B_V_DIM], pv_l[B_V_DIM:B_V_DIM + 1]
        if old is None:
            return m_new, lsum, pv
        return m_new, alpha * l_old + lsum, alpha * acc_old + pv

    def task_start(t):
        return (pl.multiple_of(tasks_ref[0, t] * TQ, TQ), pl.multiple_of(tasks_ref[1, t] * TQ, TQ))

    diag_chains = [(i, ch) for i in range(n_tiles) for ch in range(n_chains)]

    def diag_scores(pos):
        i, ch = diag_chains[pos]
        return scores(i * TQ, i * TQ, ch, ch // 2 + 1)

    q0, k0 = task_start(0)
    pending = [diag_scores(pos) for pos in range(QK_AHEAD)]
    for pos, (i, ch) in enumerate(diag_chains):
        ahead = pos + QK_AHEAD
        if ahead < len(diag_chains):
            pending.append(diag_scores(ahead))
        else:
            s_ref[ahead - len(diag_chains)] = scores(q0, k0, ahead - len(diag_chains), N_SUB)
        a = ch // 2
        m_new, l_new, acc_new = softmax_pv(pending.pop(0), i * TQ, a + 1, 0.0, None, a)
        m_ref[i * n_chains + ch] = m_new
        l_ref[i * n_chains + ch] = l_new
        acc_ref[i * n_chains + ch] = acc_new

    def off_diag(t, carry):
        q_start, key_start = task_start(t)
        nq_start, nkey_start = task_start(jnp.minimum(t + 1, n_tasks - 1))
        base = tasks_ref[0, t] * n_chains
        shift = slope * (key_start - q_start).astype(F32)
        pending = [s_ref[ch] for ch in range(QK_AHEAD)]
        olds = [(m_ref[base + ch], l_ref[base + ch], acc_ref[base + ch]) for ch in range(n_chains)]
        news, prefetched = [], []
        for ch in range(n_chains):
            ahead = ch + QK_AHEAD
            if ahead < n_chains:
                pending.append(scores(q_start, key_start, ahead, N_SUB))
            else:
                prefetched.append(scores(nq_start, nkey_start, ahead - n_chains, N_SUB))
            news.append(softmax_pv(pending.pop(0), key_start, N_SUB, shift, olds[ch], None))
        for ch, (m_new, l_new, acc_new) in enumerate(news):
            m_ref[base + ch] = m_new
            l_ref[base + ch] = l_new
            acc_ref[base + ch] = acc_new
        for n, s in enumerate(prefetched):
            s_ref[n] = s
        return carry

    lax.fori_loop(0, n_tasks, off_diag, 0)

    lp = lam_ref[...]
    lam = (jnp.exp(jnp.sum(lp[0:1] * lp[1:2], axis=-1, keepdims=True))
           - jnp.exp(jnp.sum(lp[2:3] * lp[3:4], axis=-1, keepdims=True)) + lambda_init)
    eye = jnp.where(r == c, 1.0, 0.0).astype(BF16)
    for blk in range(n_tiles * N_SUB):
        ot = acc_ref[2 * blk] / l_ref[2 * blk] - lam * (acc_ref[2 * blk + 1] / l_ref[2 * blk + 1])
        inv = lax.rsqrt(jnp.mean(ot * ot, axis=0, keepdims=True) + EPS)
        on = (ot * inv * snw_ref[...] * (1.0 - lambda_init)).astype(BF16)
        o_ref[blk * SUB:(blk + 1) * SUB, :] = _dot_nt(eye, on).astype(BF16)


def _attn_call(l, slopes, qt, k, vt, lam_p, snw, *, batch, seq, lambda_init):
    m = k.shape[0]
    n_tiles = seq // TQ
    pairs = [(i, j) for i in range(n_tiles) for j in range(i)]
    tasks = jnp.array([[i for i, _ in pairs], [j for _, j in pairs]], jnp.int32)
    n_chains_total = n_tiles * 2 * N_SUB
    smem = pl.BlockSpec(memory_space=pltpu.SMEM)
    return pl.pallas_call(
        functools.partial(_attn_kernel, lambda_init=lambda_init, n_tiles=n_tiles, n_tasks=len(pairs)),
        grid=(batch, B_HEADS),
        in_specs=[
            smem, smem,
            pl.BlockSpec((B_V_DIM, seq), lambda b, h: (h, b)),
            pl.BlockSpec((seq, B_V_DIM), lambda b, h: (b, h)),
            pl.BlockSpec((B_V_DIM, seq), lambda b, h: (h, b)),
            pl.BlockSpec((None,) + lam_p.shape[1:], lambda b, h: (l, 0, 0)),
            pl.BlockSpec((None,) + snw.shape[1:], lambda b, h: (l, 0, 0)),
        ],
        out_specs=pl.BlockSpec((seq, B_V_DIM), lambda b, h: (b, h)),
        out_shape=jax.ShapeDtypeStruct((m, B_VW), BF16),
        scratch_shapes=[
            pltpu.VMEM((TQ, 2 * B_HEAD_DIM), BF16),
            pltpu.VMEM((SUB, SUB), F32),
            pltpu.VMEM((n_chains_total, 1, SUB), F32),
            pltpu.VMEM((n_chains_total, 1, SUB), F32),
            pltpu.VMEM((n_chains_total, B_V_DIM, SUB), F32),
            pltpu.VMEM((QK_AHEAD, TQ, SUB), F32),
        ],
        compiler_params=pltpu.CompilerParams(
            dimension_semantics=("arbitrary", "arbitrary"), vmem_limit_bytes=VMEM_LIMIT),
        name="diff_attn",
    )(slopes, tasks, qt, k, vt, lam_p, snw)


def _mlp_kernel(x_ref, ya_ref, yb_ref, yc_ref, n1w_ref, wg0_ref, wg1_ref, wg2_ref, gb_ref,
                wba_ref, wbb_ref, wbc_ref, wo_ref, n2w_ref, w1_ref, w2_ref, o_ref):
    x = x_ref[...]
    h = _rms(x, n1w_ref[...]).astype(BF16)
    merged = None
    branches = ((ya_ref, wba_ref, wg0_ref), (yb_ref, wbb_ref, wg1_ref), (yc_ref, wbc_ref, wg2_ref))
    for br, (y_ref, w_ref, wg_ref) in enumerate(branches):
        gate = jax.nn.sigmoid(_dot(h, wg_ref[...]) + gb_ref[br:br + 1, :])
        term = gate * _dot(y_ref[...], w_ref[...])
        merged = term if merged is None else merged + term
    x1 = x + _dot(merged.astype(BF16), wo_ref[...])
    h2 = _rms(x1, n2w_ref[...]).astype(BF16)
    acc = x1
    for cidx in range(D_FF // FF_CHUNK):
        cols = slice(cidx * FF_CHUNK, (cidx + 1) * FF_CHUNK)
        f = jnp.square(jnp.maximum(_dot(h2, w1_ref[:, cols]), 0.0)).astype(BF16)
        acc = acc + _dot(f, w2_ref[cols, :])
    o_ref[...] = acc


def _mlp_call(l, x2d, ya, yb, yc, n1w, w_in, gb, wba, wbb, wbc, wo, n2w, w1, w2):
    m = x2d.shape[0]
    tm = TM_MLP
    rows = lambda width: pl.BlockSpec((tm, width), lambda i: (i, 0))
    whole = lambda arr: _layer_block(l, arr.shape[1:])
    gate_cols = lambda br: _layer_block(l, (D_MODEL, D_MODEL), W_G // D_MODEL + br)
    return pl.pallas_call(
        _mlp_kernel,
        grid=(m // tm,),
        in_specs=[rows(D_MODEL), rows(A_WIDTH), rows(B_VW), rows(C_WIDTH),
                  whole(n1w), gate_cols(0), gate_cols(1), gate_cols(2), whole(gb),
                  whole(wba), whole(wbb), whole(wbc), whole(wo), whole(n2w), whole(w1), whole(w2)],
        out_specs=rows(D_MODEL),
        out_shape=jax.ShapeDtypeStruct((m, D_MODEL), F32),
        compiler_params=pltpu.CompilerParams(
            dimension_semantics=("arbitrary",), vmem_limit_bytes=VMEM_LIMIT),
        name="merge_mlp",
    )(x2d, ya, yb, yc, n1w, w_in, w_in, w_in, gb, wba, wbb, wbc, wo, n2w, w1, w2)


def kernel(x, norm1_w, w_in, gate_b, a_vnorm_w, a_ws, a_bs, b_qnorm_w, b_knorm_w, b_lam, b_subnorm_w,
           c_conv_w, w_br_a, w_br_b, w_br_c, w_out, norm2_w, w_ff1, w_ff2):
    batch, seq, d = x.shape
    depth = w_in.shape[0]
    slopes = 2.0 ** (-8.0 * jnp.arange(1, B_HEADS + 1, dtype=F32) / B_HEADS)
    w_in_b, w_ff1_b, w_ff2_b, w_out_b = (w.astype(BF16) for w in (w_in, w_ff1, w_ff2, w_out))
    w_br_a_b, w_br_b_b, w_br_c_b = (w.astype(BF16) for w in (w_br_a, w_br_b, w_br_c))
    n1w = norm1_w.reshape(depth, 1, d)
    n2w = norm2_w.reshape(depth, 1, d)
    vnw = a_vnorm_w.reshape(depth, 1, A_WIDTH)
    bst = a_bs.transpose(0, 2, 1)
    qnw = jnp.tile(b_qnorm_w.reshape(depth, 1, 2 * B_HEAD_DIM), (1, B_HEADS, 1)).reshape(depth, B_QK, 1)
    knw = jnp.tile(b_knorm_w.reshape(depth, 1, 2 * B_HEAD_DIM), (1, B_HEADS, 1)).reshape(depth, 1, B_QK)
    snw = b_subnorm_w.reshape(depth, B_V_DIM, 1)
    x2d = x.reshape(batch * seq, d)
    for l in range(depth):
        lambda_init = 0.8 - 0.6 * math.exp(-0.3 * l)
        ya, qt, k, vt, yc = _proj_call(l, x2d, n1w, w_in_b, vnw, a_ws, bst, qnw, knw, c_conv_w, seq=seq)
        yb = _attn_call(l, slopes, qt, k, vt, b_lam, snw, batch=batch, seq=seq, lambda_init=lambda_init)
        x2d = _mlp_call(l, x2d, ya, yb, yc, n1w, w_in_b, gate_b, w_br_a_b, w_br_b_b, w_br_c_b, w_out_b,
                        n2w, w_ff1_b, w_ff2_b)
    return x2d.reshape(batch, seq, d)
```

```python
import functools
import math

import jax
import jax.numpy as jnp
from jax import lax
from jax.experimental import pallas as pl
from jax.experimental.pallas import tpu as pltpu

D_MODEL = 1024
A_WIDTH = 512
A_GROUPS = 4
A_BLOCK = 128
B_HEADS = 4
B_HEAD_DIM = 64
B_V_DIM = 128
B_QK = 512
B_VW = 512
C_WIDTH = 512
CONV_K = 3
N_BRANCH = 3
D_FF = 4 * D_MODEL
CHUNK = 64
EPS = 1e-6
LOG2E = math.log2(math.e)
NEG_BIG = -1e30

TM_PROJ = 1024
TQ = 1024
SUB = 256
N_SUB = TQ // SUB
QK_AHEAD = 3
S_SLOTS = 4
POS_RADIX = 32
SLOPE_PIECES = 3
ONES_ROWS = 16
TM_MLP = 512
FF_CHUNK = 1024
CARRY_ROWS = 8
WT_CHUNK = 256
W_A, W_Q, W_K, W_V, W_C, W_G = 0, 1024, 1536, 2048, 2560, 4096
VMEM_LIMIT = 56 * 1024 * 1024

BF16 = jnp.bfloat16
F32 = jnp.float32


def _dot(a, b):
    return jnp.dot(a, b, preferred_element_type=F32)


def _dot_nt(a, b):
    return lax.dot_general(a, b, (((1,), (1,)), ((), ())), preferred_element_type=F32)


def _rms(x, w):
    return x * lax.rsqrt(jnp.mean(x * x, axis=-1, keepdims=True) + EPS) * w


def _proj_kernel(x_ref, n1w_ref, wa_ref, wq_ref, wk_ref, wv_ref, wcb_ref, wcc_ref, wcx_ref,
                 vnw_ref, ws_ref, bst_ref, qnw_ref, knw_ref, cw_ref,
                 ya_ref, qt_ref, k_ref, vt_ref, yc_ref, zc_ref, wqvt_ref, *, tiles_per_seq):
    i = pl.program_id(0)
    tm = x_ref.shape[0]

    @pl.when(i % tiles_per_seq == 0)
    def _():
        zc_ref[...] = jnp.zeros_like(zc_ref)

    @pl.when(i == 0)
    def _():
        for n, w_ref in enumerate((wq_ref, wv_ref)):
            for cb in range(B_QK // WT_CHUNK):
                cols = slice(cb * WT_CHUNK, (cb + 1) * WT_CHUNK)
                wqvt_ref[n * B_QK + cb * WT_CHUNK:n * B_QK + (cb + 1) * WT_CHUNK, :] = (
                    w_ref[:, cols].astype(F32).T.astype(BF16))

    h = _rms(x_ref[...], n1w_ref[...]).astype(BF16)

    pa = _dot(h, wa_ref[...])
    pcb = _dot(h, wcb_ref[...])
    z = _dot(h, wcc_ref[...]) * _dot(h, wcx_ref[...])

    u = jax.nn.gelu(pa[:, :A_WIDTH])
    v = _rms(jax.nn.gelu(pa[:, A_WIDTH:]), vnw_ref[...]).astype(BF16)
    r = lax.broadcasted_iota(jnp.int32, (A_BLOCK, A_BLOCK), 0)
    c = lax.broadcasted_iota(jnp.int32, (A_BLOCK, A_BLOCK), 1)
    tril = c <= r
    gd = A_WIDTH // A_GROUPS
    for g in range(A_GROUPS):
        ws = jnp.where(tril, ws_ref[g], 0.0).astype(BF16)
        bias = bst_ref[:, g:g + 1]
        for blk in range(tm // A_BLOCK):
            rows = slice(blk * A_BLOCK, (blk + 1) * A_BLOCK)
            cols = slice(g * gd, (g + 1) * gd)
            sv = _dot(ws, v[rows, cols]) + bias
            ya_ref[rows, cols] = (u[rows, cols] * sv).astype(BF16)

    pqv = _dot_nt(wqvt_ref[...], h)

    row = lax.broadcasted_iota(jnp.int32, z.shape, 0)
    last1 = zc_ref[CARRY_ROWS - 1:CARRY_ROWS, :]
    last2 = zc_ref[CARRY_ROWS - 2:CARRY_ROWS - 1, :]
    z1 = jnp.where(row == 0, last1, pltpu.roll(z, 1, axis=0))
    z2 = jnp.where(row == 0, last2, jnp.where(row == 1, last1, pltpu.roll(z, 2, axis=0)))
    conv = cw_ref[0:1, :] * z2 + cw_ref[1:2, :] * z1 + cw_ref[2:3, :] * z
    yc_ref[...] = (pcb * conv).astype(BF16)
    zc_ref[...] = z[tm - CARRY_ROWS:, :]

    pk = _dot(h, wk_ref[...])

    q3 = pqv[:B_QK].reshape(B_QK // B_HEAD_DIM, B_HEAD_DIM, tm)
    qs = jnp.mean(q3 * q3, axis=1, keepdims=True)
    qn = q3 * lax.rsqrt(qs + EPS) * qnw_ref[...].reshape(B_QK // B_HEAD_DIM, B_HEAD_DIM, 1)
    qt_ref[...] = (qn * (B_HEAD_DIM ** -0.5 * LOG2E)).reshape(B_QK, tm).astype(BF16)
    vt_ref[...] = pqv[B_QK:].astype(BF16)

    gr = lax.broadcasted_iota(jnp.int32, (256, 256), 0) // B_HEAD_DIM
    gc = lax.broadcasted_iota(jnp.int32, (256, 256), 1) // B_HEAD_DIM
    ones_bd = jnp.where(gr == gc, 1.0, 0.0).astype(BF16)
    for j in range(B_QK // 256):
        cols = slice(j * 256, (j + 1) * 256)
        t = pk[:, cols]
        ss = _dot((t * t).astype(BF16), ones_bd) * (1.0 / B_HEAD_DIM)
        k_ref[:, cols] = (t * lax.rsqrt(ss + EPS) * knw_ref[:, cols]).astype(BF16)


def _layer_block(l, shape, col=0):
    nd = len(shape)
    return pl.BlockSpec((None,) + tuple(shape), lambda *_: (l,) + (0,) * (nd - 1) + (col,),
                        pipeline_mode=pl.Buffered(1))


def _proj_call(l, x2d, n1w, w_in, vnw, ws, bst, qnw, knw, cw, *, seq):
    m = x2d.shape[0]
    tm = TM_PROJ
    row_bf = lambda width: pl.BlockSpec((tm, width), lambda i: (i, 0))
    col_bf = lambda height: pl.BlockSpec((height, tm), lambda i: (0, i))
    w_cols = lambda width, start: _layer_block(l, (D_MODEL, width), start // width)
    return pl.pallas_call(
        functools.partial(_proj_kernel, tiles_per_seq=seq // tm),
        grid=(m // tm,),
        in_specs=[
            pl.BlockSpec((tm, D_MODEL), lambda i: (i, 0)),
            _layer_block(l, n1w.shape[1:]),
            w_cols(2 * A_WIDTH, W_A), w_cols(B_QK, W_Q), w_cols(B_QK, W_K), w_cols(B_VW, W_V),
            w_cols(C_WIDTH, W_C), w_cols(C_WIDTH, W_C + C_WIDTH), w_cols(C_WIDTH, W_C + 2 * C_WIDTH),
            _layer_block(l, vnw.shape[1:]), _layer_block(l, ws.shape[1:]), _layer_block(l, bst.shape[1:]),
            _layer_block(l, qnw.shape[1:]), _layer_block(l, knw.shape[1:]), _layer_block(l, cw.shape[1:]),
        ],
        out_specs=[row_bf(A_WIDTH), col_bf(B_QK), row_bf(B_QK), col_bf(B_VW), row_bf(C_WIDTH)],
        out_shape=[
            jax.ShapeDtypeStruct((m, A_WIDTH), BF16),
            jax.ShapeDtypeStruct((B_QK, m), BF16),
            jax.ShapeDtypeStruct((m, B_QK), BF16),
            jax.ShapeDtypeStruct((B_VW, m), BF16),
            jax.ShapeDtypeStruct((m, C_WIDTH), BF16),
        ],
        scratch_shapes=[pltpu.VMEM((CARRY_ROWS, C_WIDTH), F32),
                        pltpu.VMEM((B_QK + B_VW, D_MODEL), BF16)],
        compiler_params=pltpu.CompilerParams(
            dimension_semantics=("arbitrary",), vmem_limit_bytes=VMEM_LIMIT),
        name="in_proj",
    )(x2d, n1w, w_in, w_in, w_in, w_in, w_in, w_in, w_in, vnw, ws, bst, qnw, knw, cw)


def _attn_kernel(slopes_ref, tasks_ref, qt_ref, k_ref, vt_ref, lam_ref, snw_ref, o_ref,
                 kfeat_ref, diag_ref, m_ref, l_ref, acc_ref, s_ref, *, lambda_init, n_tiles, n_tasks):
    hd = pl.program_id(1)
    slope = slopes_ref[hd] * LOG2E
    n_chains = 2 * N_SUB

    r = lax.broadcasted_iota(jnp.int32, (SUB, SUB), 0)
    c = lax.broadcasted_iota(jnp.int32, (SUB, SUB), 1)
    visible = (r // CHUNK) <= (c // CHUNK)
    diag_ref[...] = jnp.where(visible, jnp.minimum(2.0 * slope * (c - r).astype(F32), 0.0), NEG_BIG)
    kr = lax.broadcasted_iota(jnp.int32, kfeat_ref.shape, 0)
    kl = lax.broadcasted_iota(jnp.int32, kfeat_ref.shape, 1)
    feat = jnp.where(kl < SLOPE_PIECES, kr // POS_RADIX, kr % POS_RADIX)
    kfeat_ref[...] = jnp.where(kl < 2 * SLOPE_PIECES, feat, 0).astype(F32).astype(BF16)

    pieces = []
    rest = jnp.full((2 * B_HEAD_DIM, SUB), slope, F32)
    for _ in range(SLOPE_PIECES):
        piece = rest.astype(BF16).astype(F32)
        pieces.append(piece)
        rest = rest - piece
    frow = lax.broadcasted_iota(jnp.int32, (2 * B_HEAD_DIM, SUB), 0)
    piece = functools.reduce(lambda acc, n: jnp.where(frow % SLOPE_PIECES == n, pieces[n], acc),
                             range(1, SLOPE_PIECES), pieces[0])
    q_feat = jnp.where(frow < SLOPE_PIECES, float(POS_RADIX) * piece,
                       jnp.where(frow < 2 * SLOPE_PIECES, piece, 0.0)).astype(BF16)

    first = frow < B_HEAD_DIM

    def scores(q_start, key_start, ch, nk):
        a, mp = divmod(ch, 2)
        rows = nk * SUB
        kk = jnp.concatenate([k_ref[pl.ds(key_start, rows), :], kfeat_ref[0:rows, :]], axis=1)
        q = qt_ref[:, pl.ds(q_start + a * SUB, SUB)]
        q = jnp.where(first, q, jnp.zeros_like(q)) if mp == 0 else jnp.where(first, jnp.zeros_like(q), q)
        return _dot(kk, jnp.concatenate([q, q_feat], axis=0))

    def softmax_pv(slot, key_start, nk, shift, old, diag_block):
        ts = [s_ref[slot, ks * SUB:(ks + 1) * SUB, :] for ks in range(nk)]
        if diag_block is not None:
            ts[diag_block] = ts[diag_block] + diag_ref[...]
        mb = functools.reduce(jnp.maximum, [jnp.max(t, axis=0, keepdims=True) for t in ts])
        if old is None:
            m_new = mb
        else:
            m_old, l_old, acc_old = old
            m_new = jnp.maximum(m_old, mb + shift)
            alpha = jnp.exp2(m_old - m_new)
        mm = m_new - shift
        pb = jnp.concatenate([jnp.exp2(t - mm).astype(BF16) for t in ts], axis=0)
        vv = jnp.concatenate([vt_ref[:, pl.ds(key_start, nk * SUB)],
                              jnp.ones((ONES_ROWS, nk * SUB), BF16)], axis=0)
        pv_l = _dot(vv, pb)
        pv, lsum = pv_l[:B_V_DIM], pv_l[B_V_DIM:B_V_DIM + 1]
        if old is None:
            return m_new, lsum, pv
        return m_new, alpha * l_old + lsum, alpha * acc_old + pv

    def task_start(t):
        return (pl.multiple_of(tasks_ref[0, t] * TQ, TQ), pl.multiple_of(tasks_ref[1, t] * TQ, TQ))

    diag_chains = [(i, ch) for i in range(n_tiles) for ch in range(n_chains)]
    q0, k0 = task_start(0)

    def diag_scores(pos):
        i, ch = diag_chains[pos]
        nk = ch // 2 + 1
        s_ref[pos % S_SLOTS, 0:nk * SUB, :] = scores(i * TQ, i * TQ, ch, nk)

    for pos in range(QK_AHEAD):
        diag_scores(pos)
    for pos, (i, ch) in enumerate(diag_chains):
        ahead = pos + QK_AHEAD
        if ahead < len(diag_chains):
            diag_scores(ahead)
        else:
            s_ref[ahead % S_SLOTS] = scores(q0, k0, ahead - len(diag_chains), N_SUB)
        a = ch // 2
        m_new, l_new, acc_new = softmax_pv(pos % S_SLOTS, i * TQ, a + 1, 0.0, None, a)
        m_ref[i * n_chains + ch] = m_new
        l_ref[i * n_chains + ch] = l_new
        acc_ref[i * n_chains + ch] = acc_new

    def off_diag(t, carry):
        q_start, key_start = task_start(t)
        nq_start, nkey_start = task_start(jnp.minimum(t + 1, n_tasks - 1))
        base = tasks_ref[0, t] * n_chains
        shift = slope * (key_start - q_start).astype(F32)
        olds = [(m_ref[base + ch], l_ref[base + ch], acc_ref[base + ch]) for ch in range(n_chains)]
        news = []
        for ch in range(n_chains):
            ahead = ch + QK_AHEAD
            if ahead < n_chains:
                s_ref[ahead % S_SLOTS] = scores(q_start, key_start, ahead, N_SUB)
            else:
                s_ref[ahead % S_SLOTS] = scores(nq_start, nkey_start, ahead - n_chains, N_SUB)
            news.append(softmax_pv(ch % S_SLOTS, key_start, N_SUB, shift, olds[ch], None))
        for ch, (m_new, l_new, acc_new) in enumerate(news):
            m_ref[base + ch] = m_new
            l_ref[base + ch] = l_new
            acc_ref[base + ch] = acc_new
        return carry

    lax.fori_loop(0, n_tasks, off_diag, 0)

    lp = lam_ref[...]
    lam = (jnp.exp(jnp.sum(lp[0:1] * lp[1:2], axis=-1, keepdims=True))
           - jnp.exp(jnp.sum(lp[2:3] * lp[3:4], axis=-1, keepdims=True)) + lambda_init)
    eye = jnp.where(r == c, 1.0, 0.0).astype(BF16)
    for blk in range(n_tiles * N_SUB):
        ot = acc_ref[2 * blk] / l_ref[2 * blk] - lam * (acc_ref[2 * blk + 1] / l_ref[2 * blk + 1])
        inv = lax.rsqrt(jnp.mean(ot * ot, axis=0, keepdims=True) + EPS)
        on = (ot * inv * snw_ref[...] * (1.0 - lambda_init)).astype(BF16)
        o_ref[blk * SUB:(blk + 1) * SUB, :] = _dot_nt(eye, on).astype(BF16)


def _attn_call(l, slopes, qt, k, vt, lam_p, snw, *, batch, seq, lambda_init):
    m = k.shape[0]
    n_tiles = seq // TQ
    pairs = [(i, j) for i in range(n_tiles) for j in range(i)]
    tasks = jnp.array([[i for i, _ in pairs], [j for _, j in pairs]], jnp.int32)
    n_chains_total = n_tiles * 2 * N_SUB
    assert n_tiles >= 2 and QK_AHEAD < S_SLOTS and (2 * N_SUB) % S_SLOTS == 0
    smem = pl.BlockSpec(memory_space=pltpu.SMEM)
    return pl.pallas_call(
        functools.partial(_attn_kernel, lambda_init=lambda_init, n_tiles=n_tiles, n_tasks=len(pairs)),
        grid=(batch, B_HEADS),
        in_specs=[
            smem, smem,
            pl.BlockSpec((B_V_DIM, seq), lambda b, h: (h, b)),
            pl.BlockSpec((seq, B_V_DIM), lambda b, h: (b, h)),
            pl.BlockSpec((B_V_DIM, seq), lambda b, h: (h, b)),
            pl.BlockSpec((None,) + lam_p.shape[1:], lambda b, h: (l, 0, 0)),
            pl.BlockSpec((None,) + snw.shape[1:], lambda b, h: (l, 0, 0)),
        ],
        out_specs=pl.BlockSpec((seq, B_V_DIM), lambda b, h: (b, h)),
        out_shape=jax.ShapeDtypeStruct((m, B_VW), BF16),
        scratch_shapes=[
            pltpu.VMEM((TQ, 2 * B_HEAD_DIM), BF16),
            pltpu.VMEM((SUB, SUB), F32),
            pltpu.VMEM((n_chains_total, 1, SUB), F32),
            pltpu.VMEM((n_chains_total, 1, SUB), F32),
            pltpu.VMEM((n_chains_total, B_V_DIM, SUB), F32),
            pltpu.VMEM((S_SLOTS, TQ, SUB), F32),
        ],
        compiler_params=pltpu.CompilerParams(
            dimension_semantics=("arbitrary", "arbitrary"), vmem_limit_bytes=VMEM_LIMIT),
        name="diff_attn",
    )(slopes, tasks, qt, k, vt, lam_p, snw)


def _mlp_kernel(x_ref, ya_ref, yb_ref, yc_ref, n1w_ref, wg0_ref, wg1_ref, wg2_ref, gb_ref,
                wba_ref, wbb_ref, wbc_ref, wo_ref, n2w_ref, w1_ref, w2_ref, o_ref):
    x = x_ref[...]
    h = _rms(x, n1w_ref[...]).astype(BF16)
    merged = None
    branches = ((ya_ref, wba_ref, wg0_ref), (yb_ref, wbb_ref, wg1_ref), (yc_ref, wbc_ref, wg2_ref))
    for br, (y_ref, w_ref, wg_ref) in enumerate(branches):
        gate = jax.nn.sigmoid(_dot(h, wg_ref[...]) + gb_ref[br:br + 1, :])
        term = gate * _dot(y_ref[...], w_ref[...])
        merged = term if merged is None else merged + term
    x1 = x + _dot(merged.astype(BF16), wo_ref[...])
    h2 = _rms(x1, n2w_ref[...]).astype(BF16)
    acc = x1
    for cidx in range(D_FF // FF_CHUNK):
        cols = slice(cidx * FF_CHUNK, (cidx + 1) * FF_CHUNK)
        f = jnp.square(jnp.maximum(_dot(h2, w1_ref[:, cols]), 0.0)).astype(BF16)
        acc = acc + _dot(f, w2_ref[cols, :])
    o_ref[...] = acc


def _mlp_call(l, x2d, ya, yb, yc, n1w, w_in, gb, wba, wbb, wbc, wo, n2w, w1, w2):
    m = x2d.shape[0]
    tm = TM_MLP
    rows = lambda width: pl.BlockSpec((tm, width), lambda i: (i, 0))
    whole = lambda arr: _layer_block(l, arr.shape[1:])
    gate_cols = lambda br: _layer_block(l, (D_MODEL, D_MODEL), W_G // D_MODEL + br)
    return pl.pallas_call(
        _mlp_kernel,
        grid=(m // tm,),
        in_specs=[rows(D_MODEL), rows(A_WIDTH), rows(B_VW), rows(C_WIDTH),
                  whole(n1w), gate_cols(0), gate_cols(1), gate_cols(2), whole(gb),
                  whole(wba), whole(wbb), whole(wbc), whole(wo), whole(n2w), whole(w1), whole(w2)],
        out_specs=rows(D_MODEL),
        out_shape=jax.ShapeDtypeStruct((m, D_MODEL), F32),
        compiler_params=pltpu.CompilerParams(
            dimension_semantics=("arbitrary",), vmem_limit_bytes=VMEM_LIMIT),
        name="merge_mlp",
    )(x2d, ya, yb, yc, n1w, w_in, w_in, w_in, gb, wba, wbb, wbc, wo, n2w, w1, w2)


def kernel(x, norm1_w, w_in, gate_b, a_vnorm_w, a_ws, a_bs, b_qnorm_w, b_knorm_w, b_lam, b_subnorm_w,
           c_conv_w, w_br_a, w_br_b, w_br_c, w_out, norm2_w, w_ff1, w_ff2):
    batch, seq, d = x.shape
    depth = w_in.shape[0]
    slopes = 2.0 ** (-8.0 * jnp.arange(1, B_HEADS + 1, dtype=F32) / B_HEADS)
    w_in_b, w_ff1_b, w_ff2_b, w_out_b = (w.astype(BF16) for w in (w_in, w_ff1, w_ff2, w_out))
    w_br_a_b, w_br_b_b, w_br_c_b = (w.astype(BF16) for w in (w_br_a, w_br_b, w_br_c))
    n1w = norm1_w.reshape(depth, 1, d)
    n2w = norm2_w.reshape(depth, 1, d)
    vnw = a_vnorm_w.reshape(depth, 1, A_WIDTH)
    bst = a_bs.transpose(0, 2, 1)
    qnw = jnp.tile(b_qnorm_w.reshape(depth, 1, 2 * B_HEAD_DIM), (1, B_HEADS, 1)).reshape(depth, B_QK, 1)
    knw = jnp.tile(b_knorm_w.reshape(depth, 1, 2 * B_HEAD_DIM), (1, B_HEADS, 1)).reshape(depth, 1, B_QK)
    snw = b_subnorm_w.reshape(depth, B_V_DIM, 1)
    x2d = x.reshape(batch * seq, d)
    for l in range(depth):
        lambda_init = 0.8 - 0.6 * math.exp(-0.3 * l)
        ya, qt, k, vt, yc = _proj_call(l, x2d, n1w, w_in_b, vnw, a_ws, bst, qnw, knw, c_conv_w, seq=seq)
        yb = _attn_call(l, slopes, qt, k, vt, b_lam, snw, batch=batch, seq=seq, lambda_init=lambda_init)
        x2d = _mlp_call(l, x2d, ya, yb, yc, n1w, w_in_b, gate_b, w_br_a_b, w_br_b_b, w_br_c_b, w_out_b,
                        n2w, w_ff1_b, w_ff2_b)
    return x2d.reshape(batch, seq, d)
```

```python
import functools
import math

import jax
import jax.numpy as jnp
from jax import lax
from jax.experimental import pallas as pl
from jax.experimental.pallas import tpu as pltpu

D_MODEL = 1024
A_WIDTH = 512
A_GROUPS = 4
A_BLOCK = 128
B_HEADS = 4
B_HEAD_DIM = 64
B_V_DIM = 128
B_QK = 512
B_VW = 512
C_WIDTH = 512
CONV_K = 3
N_BRANCH = 3
D_FF = 4 * D_MODEL
CHUNK = 64
EPS = 1e-6
LOG2E = math.log2(math.e)
NEG_BIG = -1e30

TM_PROJ = 1024
TQ = 1024
SUB = 256
N_SUB = TQ // SUB
QK_AHEAD = 3
S_SLOTS = 4
POS_RADIX = 32
SLOPE_PIECES = 3
ONES_ROWS = 16
TM_MLP = 512
FF_CHUNK = 1024
PROJ_SUB = 256
CARRY_ROWS = 8
WT_CHUNK = 256
W_A, W_Q, W_K, W_V, W_C, W_G = 0, 1024, 1536, 2048, 2560, 4096
VMEM_LIMIT = 56 * 1024 * 1024

BF16 = jnp.bfloat16
F32 = jnp.float32


def _dot(a, b):
    return jnp.dot(a, b, preferred_element_type=F32)


def _dot_nt(a, b):
    return lax.dot_general(a, b, (((1,), (1,)), ((), ())), preferred_element_type=F32)


def _rms(x, w):
    return x * lax.rsqrt(jnp.mean(x * x, axis=-1, keepdims=True) + EPS) * w


def _proj_kernel(x_ref, n1w_ref, wa_ref, wq_ref, wk_ref, wv_ref, wcb_ref, wcc_ref, wcx_ref,
                 vnw_ref, ws_ref, bst_ref, qnw_ref, knw_ref, cw_ref,
                 ya_ref, qt_ref, k_ref, vt_ref, yc_ref, zc_ref, wqvt_ref, *, tiles_per_seq):
    i = pl.program_id(0)
    tm = x_ref.shape[0]

    @pl.when(i % tiles_per_seq == 0)
    def _():
        zc_ref[...] = jnp.zeros_like(zc_ref)

    @pl.when(i == 0)
    def _():
        for n, w_ref in enumerate((wq_ref, wv_ref)):
            for cb in range(B_QK // WT_CHUNK):
                cols = slice(cb * WT_CHUNK, (cb + 1) * WT_CHUNK)
                wqvt_ref[n * B_QK + cb * WT_CHUNK:n * B_QK + (cb + 1) * WT_CHUNK, :] = (
                    w_ref[:, cols].astype(F32).T.astype(BF16))

    r = lax.broadcasted_iota(jnp.int32, (A_BLOCK, A_BLOCK), 0)
    c = lax.broadcasted_iota(jnp.int32, (A_BLOCK, A_BLOCK), 1)
    gd = A_WIDTH // A_GROUPS
    ws_tril = [jnp.where(c <= r, ws_ref[g], 0.0).astype(BF16) for g in range(A_GROUPS)]
    gr = lax.broadcasted_iota(jnp.int32, (256, 256), 0) // B_HEAD_DIM
    gc = lax.broadcasted_iota(jnp.int32, (256, 256), 1) // B_HEAD_DIM
    ones_bd = jnp.where(gr == gc, 1.0, 0.0).astype(BF16)
    row = lax.broadcasted_iota(jnp.int32, (PROJ_SUB, C_WIDTH), 0)
    z_prev = zc_ref[...]

    def project(sub):
        rs = slice(sub * PROJ_SUB, (sub + 1) * PROJ_SUB)
        h = _rms(x_ref[rs, :], n1w_ref[...]).astype(BF16)
        return dict(pa=_dot(h, wa_ref[...]), pcb=_dot(h, wcb_ref[...]),
                    z=_dot(h, wcc_ref[...]) * _dot(h, wcx_ref[...]),
                    pqv=_dot_nt(wqvt_ref[...], h), pk=_dot(h, wk_ref[...]))

    def finish(sub, p, z_prev):
        rs = slice(sub * PROJ_SUB, (sub + 1) * PROJ_SUB)
        u = jax.nn.gelu(p["pa"][:, :A_WIDTH])
        v = _rms(jax.nn.gelu(p["pa"][:, A_WIDTH:]), vnw_ref[...]).astype(BF16)
        for g in range(A_GROUPS):
            bias = bst_ref[:, g:g + 1]
            for blk in range(PROJ_SUB // A_BLOCK):
                rows = slice(blk * A_BLOCK, (blk + 1) * A_BLOCK)
                cols = slice(g * gd, (g + 1) * gd)
                sv = _dot(ws_tril[g], v[rows, cols]) + bias
                ya_ref[sub * PROJ_SUB + blk * A_BLOCK:sub * PROJ_SUB + (blk + 1) * A_BLOCK, cols] = (
                    u[rows, cols] * sv).astype(BF16)

        z = p["z"]
        last1 = z_prev[CARRY_ROWS - 1:CARRY_ROWS, :]
        last2 = z_prev[CARRY_ROWS - 2:CARRY_ROWS - 1, :]
        z1 = jnp.where(row == 0, last1, pltpu.roll(z, 1, axis=0))
        z2 = jnp.where(row == 0, last2, jnp.where(row == 1, last1, pltpu.roll(z, 2, axis=0)))
        conv = cw_ref[0:1, :] * z2 + cw_ref[1:2, :] * z1 + cw_ref[2:3, :] * z
        yc_ref[rs, :] = (p["pcb"] * conv).astype(BF16)

        pqv = p["pqv"]
        q3 = pqv[:B_QK].reshape(B_QK // B_HEAD_DIM, B_HEAD_DIM, PROJ_SUB)
        qs = jnp.mean(q3 * q3, axis=1, keepdims=True)
        qn = q3 * lax.rsqrt(qs + EPS) * qnw_ref[...].reshape(B_QK // B_HEAD_DIM, B_HEAD_DIM, 1)
        qt_ref[:, rs] = (qn * (B_HEAD_DIM ** -0.5 * LOG2E)).reshape(B_QK, PROJ_SUB).astype(BF16)
        vt_ref[:, rs] = pqv[B_QK:].astype(BF16)

        for j in range(B_QK // 256):
            cols = slice(j * 256, (j + 1) * 256)
            t = p["pk"][:, cols]
            ss = _dot((t * t).astype(BF16), ones_bd) * (1.0 / B_HEAD_DIM)
            k_ref[rs, cols] = (t * lax.rsqrt(ss + EPS) * knw_ref[:, cols]).astype(BF16)
        return z[PROJ_SUB - CARRY_ROWS:, :]

    n_sub = tm // PROJ_SUB
    projected = project(0)
    for sub in range(n_sub):
        following = project(sub + 1) if sub + 1 < n_sub else None
        z_prev = finish(sub, projected, z_prev)
        projected = following

    zc_ref[...] = z_prev


def _layer_block(l, shape, col=0):
    nd = len(shape)
    return pl.BlockSpec((None,) + tuple(shape), lambda *_: (l,) + (0,) * (nd - 1) + (col,),
                        pipeline_mode=pl.Buffered(1))


def _proj_call(l, x2d, n1w, w_in, vnw, ws, bst, qnw, knw, cw, *, seq):
    m = x2d.shape[0]
    tm = TM_PROJ
    row_bf = lambda width: pl.BlockSpec((tm, width), lambda i: (i, 0))
    col_bf = lambda height: pl.BlockSpec((height, tm), lambda i: (0, i))
    w_cols = lambda width, start: _layer_block(l, (D_MODEL, width), start // width)
    return pl.pallas_call(
        functools.partial(_proj_kernel, tiles_per_seq=seq // tm),
        grid=(m // tm,),
        in_specs=[
            pl.BlockSpec((tm, D_MODEL), lambda i: (i, 0)),
            _layer_block(l, n1w.shape[1:]),
            w_cols(2 * A_WIDTH, W_A), w_cols(B_QK, W_Q), w_cols(B_QK, W_K), w_cols(B_VW, W_V),
            w_cols(C_WIDTH, W_C), w_cols(C_WIDTH, W_C + C_WIDTH), w_cols(C_WIDTH, W_C + 2 * C_WIDTH),
            _layer_block(l, vnw.shape[1:]), _layer_block(l, ws.shape[1:]), _layer_block(l, bst.shape[1:]),
            _layer_block(l, qnw.shape[1:]), _layer_block(l, knw.shape[1:]), _layer_block(l, cw.shape[1:]),
        ],
        out_specs=[row_bf(A_WIDTH), col_bf(B_QK), row_bf(B_QK), col_bf(B_VW), row_bf(C_WIDTH)],
        out_shape=[
            jax.ShapeDtypeStruct((m, A_WIDTH), BF16),
            jax.ShapeDtypeStruct((B_QK, m), BF16),
            jax.ShapeDtypeStruct((m, B_QK), BF16),
            jax.ShapeDtypeStruct((B_VW, m), BF16),
            jax.ShapeDtypeStruct((m, C_WIDTH), BF16),
        ],
        scratch_shapes=[pltpu.VMEM((CARRY_ROWS, C_WIDTH), F32),
                        pltpu.VMEM((B_QK + B_VW, D_MODEL), BF16)],
        compiler_params=pltpu.CompilerParams(
            dimension_semantics=("arbitrary",), vmem_limit_bytes=VMEM_LIMIT),
        name="in_proj",
    )(x2d, n1w, w_in, w_in, w_in, w_in, w_in, w_in, w_in, vnw, ws, bst, qnw, knw, cw)


def _attn_kernel(slopes_ref, tasks_ref, qt_ref, k_ref, vt_ref, lam_ref, snw_ref, o_ref,
                 kfeat_ref, diag_ref, m_ref, l_ref, acc_ref, s_ref, *, lambda_init, n_tiles, n_tasks):
    hd = pl.program_id(1)
    slope = slopes_ref[hd] * LOG2E
    n_chains = 2 * N_SUB

    r = lax.broadcasted_iota(jnp.int32, (SUB, SUB), 0)
    c = lax.broadcasted_iota(jnp.int32, (SUB, SUB), 1)
    visible = (r // CHUNK) <= (c // CHUNK)
    diag_ref[...] = jnp.where(visible, jnp.minimum(2.0 * slope * (c - r).astype(F32), 0.0), NEG_BIG)
    kr = lax.broadcasted_iota(jnp.int32, kfeat_ref.shape, 0)
    kl = lax.broadcasted_iota(jnp.int32, kfeat_ref.shape, 1)
    feat = jnp.where(kl < SLOPE_PIECES, kr // POS_RADIX, kr % POS_RADIX)
    kfeat_ref[...] = jnp.where(kl < 2 * SLOPE_PIECES, feat, 0).astype(F32).astype(BF16)

    pieces = []
    rest = jnp.full((2 * B_HEAD_DIM, SUB), slope, F32)
    for _ in range(SLOPE_PIECES):
        piece = rest.astype(BF16).astype(F32)
        pieces.append(piece)
        rest = rest - piece
    frow = lax.broadcasted_iota(jnp.int32, (2 * B_HEAD_DIM, SUB), 0)
    piece = functools.reduce(lambda acc, n: jnp.where(frow % SLOPE_PIECES == n, pieces[n], acc),
                             range(1, SLOPE_PIECES), pieces[0])
    q_feat = jnp.where(frow < SLOPE_PIECES, float(POS_RADIX) * piece,
                       jnp.where(frow < 2 * SLOPE_PIECES, piece, 0.0)).astype(BF16)

    first = frow < B_HEAD_DIM

    def scores(q_start, key_start, ch, nk):
        a, mp = divmod(ch, 2)
        rows = nk * SUB
        kk = jnp.concatenate([k_ref[pl.ds(key_start, rows), :], kfeat_ref[0:rows, :]], axis=1)
        q = qt_ref[:, pl.ds(q_start + a * SUB, SUB)]
        q = jnp.where(first, q, jnp.zeros_like(q)) if mp == 0 else jnp.where(first, jnp.zeros_like(q), q)
        return _dot(kk, jnp.concatenate([q, q_feat], axis=0))

    def softmax_pv(slot, key_start, nk, shift, old, diag_block):
        ts = [s_ref[slot, ks * SUB:(ks + 1) * SUB, :] for ks in range(nk)]
        if diag_block is not None:
            ts[diag_block] = ts[diag_block] + diag_ref[...]
        mb = functools.reduce(jnp.maximum, [jnp.max(t, axis=0, keepdims=True) for t in ts])
        if old is None:
            m_new = mb
        else:
            m_old, l_old, acc_old = old
            m_new = jnp.maximum(m_old, mb + shift)
            alpha = jnp.exp2(m_old - m_new)
        mm = m_new - shift
        pb = jnp.concatenate([jnp.exp2(t - mm).astype(BF16) for t in ts], axis=0)
        vv = jnp.concatenate([vt_ref[:, pl.ds(key_start, nk * SUB)],
                              jnp.ones((ONES_ROWS, nk * SUB), BF16)], axis=0)
        pv_l = _dot(vv, pb)
        pv, lsum = pv_l[:B_V_DIM], pv_l[B_V_DIM:B_V_DIM + 1]
        if old is None:
            return m_new, lsum, pv
        return m_new, alpha * l_old + lsum, alpha * acc_old + pv

    def task_start(t):
        return (pl.multiple_of(tasks_ref[0, t] * TQ, TQ), pl.multiple_of(tasks_ref[1, t] * TQ, TQ))

    diag_chains = [(i, ch) for i in range(n_tiles) for ch in range(n_chains)]
    q0, k0 = task_start(0)

    def diag_scores(pos):
        i, ch = diag_chains[pos]
        nk = ch // 2 + 1
        s_ref[pos % S_SLOTS, 0:nk * SUB, :] = scores(i * TQ, i * TQ, ch, nk)

    for pos in range(QK_AHEAD):
        diag_scores(pos)
    for pos, (i, ch) in enumerate(diag_chains):
        ahead = pos + QK_AHEAD
        if ahead < len(diag_chains):
            diag_scores(ahead)
        else:
            s_ref[ahead % S_SLOTS] = scores(q0, k0, ahead - len(diag_chains), N_SUB)
        a = ch // 2
        m_new, l_new, acc_new = softmax_pv(pos % S_SLOTS, i * TQ, a + 1, 0.0, None, a)
        m_ref[i * n_chains + ch] = m_new
        l_ref[i * n_chains + ch] = l_new
        acc_ref[i * n_chains + ch] = acc_new

    def off_diag(t, carry):
        q_start, key_start = task_start(t)
        nq_start, nkey_start = task_start(jnp.minimum(t + 1, n_tasks - 1))
        base = tasks_ref[0, t] * n_chains
        shift = slope * (key_start - q_start).astype(F32)
        olds = [(m_ref[base + ch], l_ref[base + ch], acc_ref[base + ch]) for ch in range(n_chains)]
        news = []
        for ch in range(n_chains):
            ahead = ch + QK_AHEAD
            if ahead < n_chains:
                s_ref[ahead % S_SLOTS] = scores(q_start, key_start, ahead, N_SUB)
            else:
                s_ref[ahead % S_SLOTS] = scores(nq_start, nkey_start, ahead - n_chains, N_SUB)
            news.append(softmax_pv(ch % S_SLOTS, key_start, N_SUB, shift, olds[ch], None))
        for ch, (m_new, l_new, acc_new) in enumerate(news):
            m_ref[base + ch] = m_new
            l_ref[base + ch] = l_new
            acc_ref[base + ch] = acc_new
        return carry

    lax.fori_loop(0, n_tasks, off_diag, 0)

    lp = lam_ref[...]
    lam = (jnp.exp(jnp.sum(lp[0:1] * lp[1:2], axis=-1, keepdims=True))
           - jnp.exp(jnp.sum(lp[2:3] * lp[3:4], axis=-1, keepdims=True)) + lambda_init)
    eye = jnp.where(r == c, 1.0, 0.0).astype(BF16)
    for blk in range(n_tiles * N_SUB):
        ot = acc_ref[2 * blk] / l_ref[2 * blk] - lam * (acc_ref[2 * blk + 1] / l_ref[2 * blk + 1])
        inv = lax.rsqrt(jnp.mean(ot * ot, axis=0, keepdims=True) + EPS)
        on = (ot * inv * snw_ref[...] * (1.0 - lambda_init)).astype(BF16)
        o_ref[blk * SUB:(blk + 1) * SUB, :] = _dot_nt(eye, on).astype(BF16)


def _attn_call(l, slopes, qt, k, vt, lam_p, snw, *, batch, seq, lambda_init):
    m = k.shape[0]
    n_tiles = seq // TQ
    pairs = [(i, j) for i in range(n_tiles) for j in range(i)]
    tasks = jnp.array([[i for i, _ in pairs], [j for _, j in pairs]], jnp.int32)
    n_chains_total = n_tiles * 2 * N_SUB
    assert n_tiles >= 2 and QK_AHEAD < S_SLOTS and (2 * N_SUB) % S_SLOTS == 0
    smem = pl.BlockSpec(memory_space=pltpu.SMEM)
    return pl.pallas_call(
        functools.partial(_attn_kernel, lambda_init=lambda_init, n_tiles=n_tiles, n_tasks=len(pairs)),
        grid=(batch, B_HEADS),
        in_specs=[
            smem, smem,
            pl.BlockSpec((B_V_DIM, seq), lambda b, h: (h, b)),
            pl.BlockSpec((seq, B_V_DIM), lambda b, h: (b, h)),
            pl.BlockSpec((B_V_DIM, seq), lambda b, h: (h, b)),
            pl.BlockSpec((None,) + lam_p.shape[1:], lambda b, h: (l, 0, 0)),
            pl.BlockSpec((None,) + snw.shape[1:], lambda b, h: (l, 0, 0)),
        ],
        out_specs=pl.BlockSpec((seq, B_V_DIM), lambda b, h: (b, h)),
        out_shape=jax.ShapeDtypeStruct((m, B_VW), BF16),
        scratch_shapes=[
            pltpu.VMEM((TQ, 2 * B_HEAD_DIM), BF16),
            pltpu.VMEM((SUB, SUB), F32),
            pltpu.VMEM((n_chains_total, 1, SUB), F32),
            pltpu.VMEM((n_chains_total, 1, SUB), F32),
            pltpu.VMEM((n_chains_total, B_V_DIM, SUB), F32),
            pltpu.VMEM((S_SLOTS, TQ, SUB), F32),
        ],
        compiler_params=pltpu.CompilerParams(
            dimension_semantics=("arbitrary", "arbitrary"), vmem_limit_bytes=VMEM_LIMIT),
        name="diff_attn",
    )(slopes, tasks, qt, k, vt, lam_p, snw)


def _mlp_kernel(x_ref, ya_ref, yb_ref, yc_ref, n1w_ref, wg0_ref, wg1_ref, wg2_ref, gb_ref,
                wba_ref, wbb_ref, wbc_ref, wo_ref, n2w_ref, w1_ref, w2_ref, o_ref):
    x = x_ref[...]
    h = _rms(x, n1w_ref[...]).astype(BF16)
    merged = None
    branches = ((ya_ref, wba_ref, wg0_ref), (yb_ref, wbb_ref, wg1_ref), (yc_ref, wbc_ref, wg2_ref))
    for br, (y_ref, w_ref, wg_ref) in enumerate(branches):
        gate = jax.nn.sigmoid(_dot(h, wg_ref[...]) + gb_ref[br:br + 1, :])
        term = gate * _dot(y_ref[...], w_ref[...])
        merged = term if merged is None else merged + term
    x1 = x + _dot(merged.astype(BF16), wo_ref[...])
    h2 = _rms(x1, n2w_ref[...]).astype(BF16)
    acc = x1
    for cidx in range(D_FF // FF_CHUNK):
        cols = slice(cidx * FF_CHUNK, (cidx + 1) * FF_CHUNK)
        f = jnp.square(jnp.maximum(_dot(h2, w1_ref[:, cols]), 0.0)).astype(BF16)
        acc = acc + _dot(f, w2_ref[cols, :])
    o_ref[...] = acc


def _mlp_call(l, x2d, ya, yb, yc, n1w, w_in, gb, wba, wbb, wbc, wo, n2w, w1, w2):
    m = x2d.shape[0]
    tm = TM_MLP
    rows = lambda width: pl.BlockSpec((tm, width), lambda i: (i, 0))
    whole = lambda arr: _layer_block(l, arr.shape[1:])
    gate_cols = lambda br: _layer_block(l, (D_MODEL, D_MODEL), W_G // D_MODEL + br)
    return pl.pallas_call(
        _mlp_kernel,
        grid=(m // tm,),
        in_specs=[rows(D_MODEL), rows(A_WIDTH), rows(B_VW), rows(C_WIDTH),
                  whole(n1w), gate_cols(0), gate_cols(1), gate_cols(2), whole(gb),
                  whole(wba), whole(wbb), whole(wbc), whole(wo), whole(n2w), whole(w1), whole(w2)],
        out_specs=rows(D_MODEL),
        out_shape=jax.ShapeDtypeStruct((m, D_MODEL), F32),
        compiler_params=pltpu.CompilerParams(
            dimension_semantics=("arbitrary",), vmem_limit_bytes=VMEM_LIMIT),
        name="merge_mlp",
    )(x2d, ya, yb, yc, n1w, w_in, w_in, w_in, gb, wba, wbb, wbc, wo, n2w, w1, w2)


def kernel(x, norm1_w, w_in, gate_b, a_vnorm_w, a_ws, a_bs, b_qnorm_w, b_knorm_w, b_lam, b_subnorm_w,
           c_conv_w, w_br_a, w_br_b, w_br_c, w_out, norm2_w, w_ff1, w_ff2):
    batch, seq, d = x.shape
    depth = w_in.shape[0]
    slopes = 2.0 ** (-8.0 * jnp.arange(1, B_HEADS + 1, dtype=F32) / B_HEADS)
    w_in_b, w_ff1_b, w_ff2_b, w_out_b = (w.astype(BF16) for w in (w_in, w_ff1, w_ff2, w_out))
    w_br_a_b, w_br_b_b, w_br_c_b = (w.astype(BF16) for w in (w_br_a, w_br_b, w_br_c))
    n1w = norm1_w.reshape(depth, 1, d)
    n2w = norm2_w.reshape(depth, 1, d)
    vnw = a_vnorm_w.reshape(depth, 1, A_WIDTH)
    bst = a_bs.transpose(0, 2, 1)
    qnw = jnp.tile(b_qnorm_w.reshape(depth, 1, 2 * B_HEAD_DIM), (1, B_HEADS, 1)).reshape(depth, B_QK, 1)
    knw = jnp.tile(b_knorm_w.reshape(depth, 1, 2 * B_HEAD_DIM), (1, B_HEADS, 1)).reshape(depth, 1, B_QK)
    snw = b_subnorm_w.reshape(depth, B_V_DIM, 1)
    x2d = x.reshape(batch * seq, d)
    for l in range(depth):
        lambda_init = 0.8 - 0.6 * math.exp(-0.3 * l)
        ya, qt, k, vt, yc = _proj_call(l, x2d, n1w, w_in_b, vnw, a_ws, bst, qnw, knw, c_conv_w, seq=seq)
        yb = _attn_call(l, slopes, qt, k, vt, b_lam, snw, batch=batch, seq=seq, lambda_init=lambda_init)
        x2d = _mlp_call(l, x2d, ya, yb, yc, n1w, w_in_b, gate_b, w_br_a_b, w_br_b_b, w_br_c_b, w_out_b,
                        n2w, w_ff1_b, w_ff2_b)
    return x2d.reshape(batch, seq, d)
```

```python
import functools
import math

import jax
import jax.numpy as jnp
from jax import lax
from jax.experimental import pallas as pl
from jax.experimental.pallas import tpu as pltpu

D_MODEL = 1024
A_WIDTH = 512
A_GROUPS = 4
A_BLOCK = 128
B_HEADS = 4
B_HEAD_DIM = 64
B_V_DIM = 128
B_QK = 512
B_VW = 512
C_WIDTH = 512
CONV_K = 3
N_BRANCH = 3
D_FF = 4 * D_MODEL
CHUNK = 64
EPS = 1e-6
LOG2E = math.log2(math.e)
NEG_BIG = -1e30

TM_PROJ = 1024
TQ = 1024
SUB = 256
N_SUB = TQ // SUB
QK_AHEAD = 3
S_SLOTS = 4
POS_RADIX = 32
SLOPE_PIECES = 3
ONES_ROWS = 16
TM_MLP = 512
FF_CHUNK = 1024
PROJ_SUB = 256
N_ATTN_IN = 7
N_PROJ_IN, N_PROJ_OUT = 15, 5
CARRY_ROWS = 8
WT_CHUNK = 256
W_A, W_Q, W_K, W_V, W_C, W_G = 0, 1024, 1536, 2048, 2560, 4096
VMEM_LIMIT = 56 * 1024 * 1024

BF16 = jnp.bfloat16
F32 = jnp.float32


def _dot(a, b):
    return jnp.dot(a, b, preferred_element_type=F32)


def _dot_nt(a, b):
    return lax.dot_general(a, b, (((1,), (1,)), ((), ())), preferred_element_type=F32)


def _rms(x, w):
    return x * lax.rsqrt(jnp.mean(x * x, axis=-1, keepdims=True) + EPS) * w


def _cast_specs(l, stacked, n_steps, step_of):
    in_specs, out_specs, out_shape = [], [], []
    for w in stacked:
        rows = w.shape[1] // n_steps
        in_specs.append(pl.BlockSpec((None, rows, w.shape[2]), lambda *g: (l, step_of(*g), 0)))
        out_specs.append(pl.BlockSpec((rows, w.shape[2]), lambda *g: (step_of(*g), 0)))
        out_shape.append(jax.ShapeDtypeStruct(w.shape[1:], BF16))
    return in_specs, out_specs, out_shape


def _cast_rows(src_refs, dst_refs):
    for src, dst in zip(src_refs, dst_refs):
        dst[...] = src[...].astype(BF16)


def _proj_kernel(*refs, tiles_per_seq, n_cast):
    (x_ref, n1w_ref, wa_ref, wq_ref, wk_ref, wv_ref, wcb_ref, wcc_ref, wcx_ref,
     vnw_ref, ws_ref, bst_ref, qnw_ref, knw_ref, cw_ref) = refs[:N_PROJ_IN]
    cast_src = refs[N_PROJ_IN:N_PROJ_IN + n_cast]
    ya_ref, qt_ref, k_ref, vt_ref, yc_ref = refs[N_PROJ_IN + n_cast:N_PROJ_IN + n_cast + N_PROJ_OUT]
    cast_dst = refs[N_PROJ_IN + n_cast + N_PROJ_OUT:N_PROJ_IN + 2 * n_cast + N_PROJ_OUT]
    zc_ref, wqvt_ref = refs[N_PROJ_IN + 2 * n_cast + N_PROJ_OUT:]
    i = pl.program_id(0)
    tm = x_ref.shape[0]
    _cast_rows(cast_src, cast_dst)

    @pl.when(i % tiles_per_seq == 0)
    def _():
        zc_ref[...] = jnp.zeros_like(zc_ref)

    @pl.when(i == 0)
    def _():
        for n, w_ref in enumerate((wq_ref, wv_ref)):
            for cb in range(B_QK // WT_CHUNK):
                cols = slice(cb * WT_CHUNK, (cb + 1) * WT_CHUNK)
                wqvt_ref[n * B_QK + cb * WT_CHUNK:n * B_QK + (cb + 1) * WT_CHUNK, :] = (
                    w_ref[:, cols].astype(F32).T.astype(BF16))

    r = lax.broadcasted_iota(jnp.int32, (A_BLOCK, A_BLOCK), 0)
    c = lax.broadcasted_iota(jnp.int32, (A_BLOCK, A_BLOCK), 1)
    gd = A_WIDTH // A_GROUPS
    ws_tril = [jnp.where(c <= r, ws_ref[g], 0.0).astype(BF16) for g in range(A_GROUPS)]
    gr = lax.broadcasted_iota(jnp.int32, (256, 256), 0) // B_HEAD_DIM
    gc = lax.broadcasted_iota(jnp.int32, (256, 256), 1) // B_HEAD_DIM
    ones_bd = jnp.where(gr == gc, 1.0, 0.0).astype(BF16)
    row = lax.broadcasted_iota(jnp.int32, (PROJ_SUB, C_WIDTH), 0)
    z_prev = zc_ref[...]

    def project(sub):
        rs = slice(sub * PROJ_SUB, (sub + 1) * PROJ_SUB)
        h = _rms(x_ref[rs, :], n1w_ref[...]).astype(BF16)
        return dict(pa=_dot(h, wa_ref[...]), pcb=_dot(h, wcb_ref[...]),
                    z=_dot(h, wcc_ref[...]) * _dot(h, wcx_ref[...]),
                    pqv=_dot_nt(wqvt_ref[...], h), pk=_dot(h, wk_ref[...]))

    def finish(sub, p, z_prev):
        rs = slice(sub * PROJ_SUB, (sub + 1) * PROJ_SUB)
        u = jax.nn.gelu(p["pa"][:, :A_WIDTH])
        v = _rms(jax.nn.gelu(p["pa"][:, A_WIDTH:]), vnw_ref[...]).astype(BF16)
        for g in range(A_GROUPS):
            bias = bst_ref[:, g:g + 1]
            for blk in range(PROJ_SUB // A_BLOCK):
                rows = slice(blk * A_BLOCK, (blk + 1) * A_BLOCK)
                cols = slice(g * gd, (g + 1) * gd)
                sv = _dot(ws_tril[g], v[rows, cols]) + bias
                ya_ref[sub * PROJ_SUB + blk * A_BLOCK:sub * PROJ_SUB + (blk + 1) * A_BLOCK, cols] = (
                    u[rows, cols] * sv).astype(BF16)

        z = p["z"]
        last1 = z_prev[CARRY_ROWS - 1:CARRY_ROWS, :]
        last2 = z_prev[CARRY_ROWS - 2:CARRY_ROWS - 1, :]
        z1 = jnp.where(row == 0, last1, pltpu.roll(z, 1, axis=0))
        z2 = jnp.where(row == 0, last2, jnp.where(row == 1, last1, pltpu.roll(z, 2, axis=0)))
        conv = cw_ref[0:1, :] * z2 + cw_ref[1:2, :] * z1 + cw_ref[2:3, :] * z
        yc_ref[rs, :] = (p["pcb"] * conv).astype(BF16)

        pqv = p["pqv"]
        q3 = pqv[:B_QK].reshape(B_QK // B_HEAD_DIM, B_HEAD_DIM, PROJ_SUB)
        qs = jnp.mean(q3 * q3, axis=1, keepdims=True)
        qn = q3 * lax.rsqrt(qs + EPS) * qnw_ref[...].reshape(B_QK // B_HEAD_DIM, B_HEAD_DIM, 1)
        qt_ref[:, rs] = (qn * (B_HEAD_DIM ** -0.5 * LOG2E)).reshape(B_QK, PROJ_SUB).astype(BF16)
        vt_ref[:, rs] = pqv[B_QK:].astype(BF16)

        for j in range(B_QK // 256):
            cols = slice(j * 256, (j + 1) * 256)
            t = p["pk"][:, cols]
            ss = _dot((t * t).astype(BF16), ones_bd) * (1.0 / B_HEAD_DIM)
            k_ref[rs, cols] = (t * lax.rsqrt(ss + EPS) * knw_ref[:, cols]).astype(BF16)
        return z[PROJ_SUB - CARRY_ROWS:, :]

    n_sub = tm // PROJ_SUB
    projected = project(0)
    for sub in range(n_sub):
        following = project(sub + 1) if sub + 1 < n_sub else None
        z_prev = finish(sub, projected, z_prev)
        projected = following

    zc_ref[...] = z_prev


def _layer_block(l, shape, col=0):
    nd = len(shape)
    return pl.BlockSpec((None,) + tuple(shape), lambda *_: (l,) + (0,) * (nd - 1) + (col,),
                        pipeline_mode=pl.Buffered(1))


def _proj_call(l, x2d, n1w, w_in, vnw, ws, bst, qnw, knw, cw, to_cast, *, seq):
    m = x2d.shape[0]
    tm = TM_PROJ
    cast_in, cast_out, cast_shape = _cast_specs(l, to_cast, m // tm, lambda i: i)
    row_bf = lambda width: pl.BlockSpec((tm, width), lambda i: (i, 0))
    col_bf = lambda height: pl.BlockSpec((height, tm), lambda i: (0, i))
    w_cols = lambda width, start: _layer_block(0, (D_MODEL, width), start // width)
    return pl.pallas_call(
        functools.partial(_proj_kernel, tiles_per_seq=seq // tm, n_cast=len(to_cast)),
        grid=(m // tm,),
        in_specs=[
            pl.BlockSpec((tm, D_MODEL), lambda i: (i, 0)),
            _layer_block(l, n1w.shape[1:]),
            w_cols(2 * A_WIDTH, W_A), w_cols(B_QK, W_Q), w_cols(B_QK, W_K), w_cols(B_VW, W_V),
            w_cols(C_WIDTH, W_C), w_cols(C_WIDTH, W_C + C_WIDTH), w_cols(C_WIDTH, W_C + 2 * C_WIDTH),
            _layer_block(l, vnw.shape[1:]), _layer_block(l, ws.shape[1:]), _layer_block(l, bst.shape[1:]),
            _layer_block(l, qnw.shape[1:]), _layer_block(l, knw.shape[1:]), _layer_block(l, cw.shape[1:]),
        ] + cast_in,
        out_specs=[row_bf(A_WIDTH), col_bf(B_QK), row_bf(B_QK), col_bf(B_VW), row_bf(C_WIDTH)] + cast_out,
        out_shape=[
            jax.ShapeDtypeStruct((m, A_WIDTH), BF16),
            jax.ShapeDtypeStruct((B_QK, m), BF16),
            jax.ShapeDtypeStruct((m, B_QK), BF16),
            jax.ShapeDtypeStruct((B_VW, m), BF16),
            jax.ShapeDtypeStruct((m, C_WIDTH), BF16),
        ] + cast_shape,
        scratch_shapes=[pltpu.VMEM((CARRY_ROWS, C_WIDTH), F32),
                        pltpu.VMEM((B_QK + B_VW, D_MODEL), BF16)],
        compiler_params=pltpu.CompilerParams(
            dimension_semantics=("arbitrary",), vmem_limit_bytes=VMEM_LIMIT),
        name="in_proj",
    )(x2d, n1w, w_in, w_in, w_in, w_in, w_in, w_in, w_in, vnw, ws, bst, qnw, knw, cw, *to_cast)


def _attn_kernel(*refs, lambda_init, n_tiles, n_tasks, n_cast):
    slopes_ref, tasks_ref, qt_ref, k_ref, vt_ref, lam_ref, snw_ref = refs[:N_ATTN_IN]
    cast_src = refs[N_ATTN_IN:N_ATTN_IN + n_cast]
    o_ref = refs[N_ATTN_IN + n_cast]
    cast_dst = refs[N_ATTN_IN + n_cast + 1:N_ATTN_IN + 2 * n_cast + 1]
    kfeat_ref, diag_ref, m_ref, l_ref, acc_ref, s_ref = refs[N_ATTN_IN + 2 * n_cast + 1:]
    _cast_rows(cast_src, cast_dst)
    hd = pl.program_id(1)
    slope = slopes_ref[hd] * LOG2E
    n_chains = 2 * N_SUB

    r = lax.broadcasted_iota(jnp.int32, (SUB, SUB), 0)
    c = lax.broadcasted_iota(jnp.int32, (SUB, SUB), 1)
    visible = (r // CHUNK) <= (c // CHUNK)
    diag_ref[...] = jnp.where(visible, jnp.minimum(2.0 * slope * (c - r).astype(F32), 0.0), NEG_BIG)
    kr = lax.broadcasted_iota(jnp.int32, kfeat_ref.shape, 0)
    kl = lax.broadcasted_iota(jnp.int32, kfeat_ref.shape, 1)
    feat = jnp.where(kl < SLOPE_PIECES, kr // POS_RADIX, kr % POS_RADIX)
    kfeat_ref[...] = jnp.where(kl < 2 * SLOPE_PIECES, feat, 0).astype(F32).astype(BF16)

    pieces = []
    rest = jnp.full((2 * B_HEAD_DIM, SUB), slope, F32)
    for _ in range(SLOPE_PIECES):
        piece = rest.astype(BF16).astype(F32)
        pieces.append(piece)
        rest = rest - piece
    frow = lax.broadcasted_iota(jnp.int32, (2 * B_HEAD_DIM, SUB), 0)
    piece = functools.reduce(lambda acc, n: jnp.where(frow % SLOPE_PIECES == n, pieces[n], acc),
                             range(1, SLOPE_PIECES), pieces[0])
    q_feat = jnp.where(frow < SLOPE_PIECES, float(POS_RADIX) * piece,
                       jnp.where(frow < 2 * SLOPE_PIECES, piece, 0.0)).astype(BF16)

    first = frow < B_HEAD_DIM

    def scores(q_start, key_start, ch, nk):
        a, mp = divmod(ch, 2)
        rows = nk * SUB
        kk = jnp.concatenate([k_ref[pl.ds(key_start, rows), :], kfeat_ref[0:rows, :]], axis=1)
        q = qt_ref[:, pl.ds(q_start + a * SUB, SUB)]
        q = jnp.where(first, q, jnp.zeros_like(q)) if mp == 0 else jnp.where(first, jnp.zeros_like(q), q)
        return _dot(kk, jnp.concatenate([q, q_feat], axis=0))

    def softmax_pv(slot, key_start, nk, shift, old, diag_block):
        ts = [s_ref[slot, ks * SUB:(ks + 1) * SUB, :] for ks in range(nk)]
        if diag_block is not None:
            ts[diag_block] = ts[diag_block] + diag_ref[...]
        mb = functools.reduce(jnp.maximum, [jnp.max(t, axis=0, keepdims=True) for t in ts])
        if old is None:
            m_new = mb
        else:
            m_old, l_old, acc_old = old
            m_new = jnp.maximum(m_old, mb + shift)
            alpha = jnp.exp2(m_old - m_new)
        mm = m_new - shift
        pb = jnp.concatenate([jnp.exp2(t - mm).astype(BF16) for t in ts], axis=0)
        vv = jnp.concatenate([vt_ref[:, pl.ds(key_start, nk * SUB)],
                              jnp.ones((ONES_ROWS, nk * SUB), BF16)], axis=0)
        pv_l = _dot(vv, pb)
        pv, lsum = pv_l[:B_V_DIM], pv_l[B_V_DIM:B_V_DIM + 1]
        if old is None:
            return m_new, lsum, pv
        return m_new, alpha * l_old + lsum, alpha * acc_old + pv

    def task_start(t):
        return (pl.multiple_of(tasks_ref[0, t] * TQ, TQ), pl.multiple_of(tasks_ref[1, t] * TQ, TQ))

    diag_chains = [(i, ch) for i in range(n_tiles) for ch in range(n_chains)]
    q0, k0 = task_start(0)

    def diag_scores(pos):
        i, ch = diag_chains[pos]
        nk = ch // 2 + 1
        s_ref[pos % S_SLOTS, 0:nk * SUB, :] = scores(i * TQ, i * TQ, ch, nk)

    for pos in range(QK_AHEAD):
        diag_scores(pos)
    for pos, (i, ch) in enumerate(diag_chains):
        ahead = pos + QK_AHEAD
        if ahead < len(diag_chains):
            diag_scores(ahead)
        else:
            s_ref[ahead % S_SLOTS] = scores(q0, k0, ahead - len(diag_chains), N_SUB)
        a = ch // 2
        m_new, l_new, acc_new = softmax_pv(pos % S_SLOTS, i * TQ, a + 1, 0.0, None, a)
        m_ref[i * n_chains + ch] = m_new
        l_ref[i * n_chains + ch] = l_new
        acc_ref[i * n_chains + ch] = acc_new

    def off_diag(t, carry):
        q_start, key_start = task_start(t)
        nq_start, nkey_start = task_start(jnp.minimum(t + 1, n_tasks - 1))
        base = tasks_ref[0, t] * n_chains
        shift = slope * (key_start - q_start).astype(F32)
        olds = [(m_ref[base + ch], l_ref[base + ch], acc_ref[base + ch]) for ch in range(n_chains)]
        news = []
        for ch in range(n_chains):
            ahead = ch + QK_AHEAD
            if ahead < n_chains:
                s_ref[ahead % S_SLOTS] = scores(q_start, key_start, ahead, N_SUB)
            else:
                s_ref[ahead % S_SLOTS] = scores(nq_start, nkey_start, ahead - n_chains, N_SUB)
            news.append(softmax_pv(ch % S_SLOTS, key_start, N_SUB, shift, olds[ch], None))
        for ch, (m_new, l_new, acc_new) in enumerate(news):
            m_ref[base + ch] = m_new
            l_ref[base + ch] = l_new
            acc_ref[base + ch] = acc_new
        return carry

    lax.fori_loop(0, n_tasks, off_diag, 0)

    lp = lam_ref[...]
    lam = (jnp.exp(jnp.sum(lp[0:1] * lp[1:2], axis=-1, keepdims=True))
           - jnp.exp(jnp.sum(lp[2:3] * lp[3:4], axis=-1, keepdims=True)) + lambda_init)
    eye = jnp.where(r == c, 1.0, 0.0).astype(BF16)
    for blk in range(n_tiles * N_SUB):
        ot = acc_ref[2 * blk] / l_ref[2 * blk] - lam * (acc_ref[2 * blk + 1] / l_ref[2 * blk + 1])
        inv = lax.rsqrt(jnp.mean(ot * ot, axis=0, keepdims=True) + EPS)
        on = (ot * inv * snw_ref[...] * (1.0 - lambda_init)).astype(BF16)
        o_ref[blk * SUB:(blk + 1) * SUB, :] = _dot_nt(eye, on).astype(BF16)


def _attn_call(l, slopes, qt, k, vt, lam_p, snw, to_cast, *, batch, seq, lambda_init):
    m = k.shape[0]
    cast_in, cast_out, cast_shape = _cast_specs(l + 1, to_cast, batch * B_HEADS, lambda b, h: b * B_HEADS + h)
    n_tiles = seq // TQ
    pairs = [(i, j) for i in range(n_tiles) for j in range(i)]
    tasks = jnp.array([[i for i, _ in pairs], [j for _, j in pairs]], jnp.int32)
    n_chains_total = n_tiles * 2 * N_SUB
    assert n_tiles >= 2 and QK_AHEAD < S_SLOTS and (2 * N_SUB) % S_SLOTS == 0
    smem = pl.BlockSpec(memory_space=pltpu.SMEM)
    return pl.pallas_call(
        functools.partial(_attn_kernel, lambda_init=lambda_init, n_tiles=n_tiles, n_tasks=len(pairs),
                          n_cast=len(to_cast)),
        grid=(batch, B_HEADS),
        in_specs=[
            smem, smem,
            pl.BlockSpec((B_V_DIM, seq), lambda b, h: (h, b)),
            pl.BlockSpec((seq, B_V_DIM), lambda b, h: (b, h)),
            pl.BlockSpec((B_V_DIM, seq), lambda b, h: (h, b)),
            pl.BlockSpec((None,) + lam_p.shape[1:], lambda b, h: (l, 0, 0)),
            pl.BlockSpec((None,) + snw.shape[1:], lambda b, h: (l, 0, 0)),
        ] + cast_in,
        out_specs=[pl.BlockSpec((seq, B_V_DIM), lambda b, h: (b, h))] + cast_out,
        out_shape=[jax.ShapeDtypeStruct((m, B_VW), BF16)] + cast_shape,
        scratch_shapes=[
            pltpu.VMEM((TQ, 2 * B_HEAD_DIM), BF16),
            pltpu.VMEM((SUB, SUB), F32),
            pltpu.VMEM((n_chains_total, 1, SUB), F32),
            pltpu.VMEM((n_chains_total, 1, SUB), F32),
            pltpu.VMEM((n_chains_total, B_V_DIM, SUB), F32),
            pltpu.VMEM((S_SLOTS, TQ, SUB), F32),
        ],
        compiler_params=pltpu.CompilerParams(
            dimension_semantics=("arbitrary", "arbitrary"), vmem_limit_bytes=VMEM_LIMIT),
        name="diff_attn",
    )(slopes, tasks, qt, k, vt, lam_p, snw, *to_cast)


def _mlp_kernel(x_ref, ya_ref, yb_ref, yc_ref, n1w_ref, wg0_ref, wg1_ref, wg2_ref, gb_ref,
                wba_ref, wbb_ref, wbc_ref, wo_ref, n2w_ref, w1_ref, w2_ref, o_ref):
    x = x_ref[...]
    h = _rms(x, n1w_ref[...]).astype(BF16)
    merged = None
    branches = ((ya_ref, wba_ref, wg0_ref), (yb_ref, wbb_ref, wg1_ref), (yc_ref, wbc_ref, wg2_ref))
    for br, (y_ref, w_ref, wg_ref) in enumerate(branches):
        gate = jax.nn.sigmoid(_dot(h, wg_ref[...]) + gb_ref[br:br + 1, :])
        term = gate * _dot(y_ref[...], w_ref[...])
        merged = term if merged is None else merged + term
    x1 = x + _dot(merged.astype(BF16), wo_ref[...])
    h2 = _rms(x1, n2w_ref[...]).astype(BF16)
    acc = x1
    for cidx in range(D_FF // FF_CHUNK):
        cols = slice(cidx * FF_CHUNK, (cidx + 1) * FF_CHUNK)
        f = jnp.square(jnp.maximum(_dot(h2, w1_ref[:, cols]), 0.0)).astype(BF16)
        acc = acc + _dot(f, w2_ref[cols, :])
    o_ref[...] = acc


def _mlp_call(l, x2d, ya, yb, yc, n1w, w_in, gb, wba, wbb, wbc, wo, n2w, w1, w2):
    m = x2d.shape[0]
    tm = TM_MLP
    rows = lambda width: pl.BlockSpec((tm, width), lambda i: (i, 0))
    whole = lambda arr: _layer_block(l, arr.shape[1:])
    own = lambda arr: _layer_block(0, arr.shape[1:])
    gate_cols = lambda br: _layer_block(0, (D_MODEL, D_MODEL), W_G // D_MODEL + br)
    return pl.pallas_call(
        _mlp_kernel,
        grid=(m // tm,),
        in_specs=[rows(D_MODEL), rows(A_WIDTH), rows(B_VW), rows(C_WIDTH),
                  whole(n1w), gate_cols(0), gate_cols(1), gate_cols(2), whole(gb),
                  own(wba), own(wbb), own(wbc), own(wo), whole(n2w), own(w1), own(w2)],
        out_specs=rows(D_MODEL),
        out_shape=jax.ShapeDtypeStruct((m, D_MODEL), F32),
        compiler_params=pltpu.CompilerParams(
            dimension_semantics=("arbitrary",), vmem_limit_bytes=VMEM_LIMIT),
        name="merge_mlp",
    )(x2d, ya, yb, yc, n1w, w_in, w_in, w_in, gb, wba, wbb, wbc, wo, n2w, w1, w2)


def kernel(x, norm1_w, w_in, gate_b, a_vnorm_w, a_ws, a_bs, b_qnorm_w, b_knorm_w, b_lam, b_subnorm_w,
           c_conv_w, w_br_a, w_br_b, w_br_c, w_out, norm2_w, w_ff1, w_ff2):
    batch, seq, d = x.shape
    depth = w_in.shape[0]
    slopes = 2.0 ** (-8.0 * jnp.arange(1, B_HEADS + 1, dtype=F32) / B_HEADS)
    n1w = norm1_w.reshape(depth, 1, d)
    n2w = norm2_w.reshape(depth, 1, d)
    vnw = a_vnorm_w.reshape(depth, 1, A_WIDTH)
    bst = a_bs.transpose(0, 2, 1)
    qnw = jnp.tile(b_qnorm_w.reshape(depth, 1, 2 * B_HEAD_DIM), (1, B_HEADS, 1)).reshape(depth, B_QK, 1)
    knw = jnp.tile(b_knorm_w.reshape(depth, 1, 2 * B_HEAD_DIM), (1, B_HEADS, 1)).reshape(depth, 1, B_QK)
    snw = b_subnorm_w.reshape(depth, B_V_DIM, 1)
    later_weights = (w_br_a, w_br_b, w_br_c, w_out, w_ff1, w_ff2)
    x2d = x.reshape(batch * seq, d)
    w_in_l = w_in[0].astype(BF16)
    for l in range(depth):
        lambda_init = 0.8 - 0.6 * math.exp(-0.3 * l)
        w_in_l = w_in_l[None]
        ya, qt, k, vt, yc, *later_b = _proj_call(
            l, x2d, n1w, w_in_l, vnw, a_ws, bst, qnw, knw, c_conv_w, later_weights, seq=seq)
        yb, *next_w_in = _attn_call(l, slopes, qt, k, vt, b_lam, snw, (w_in,) if l + 1 < depth else (),
                                    batch=batch, seq=seq, lambda_init=lambda_init)
        wba, wbb, wbc, wo, w1, w2 = (w[None] for w in later_b)
        x2d = _mlp_call(l, x2d, ya, yb, yc, n1w, w_in_l, gate_b, wba, wbb, wbc, wo, n2w, w1, w2)
        if next_w_in:
            w_in_l = next_w_in[0]
    return x2d.reshape(batch, seq, d)
```

```python
import functools
import math

import jax
import jax.numpy as jnp
from jax import lax
from jax.experimental import pallas as pl
from jax.experimental.pallas import tpu as pltpu

D_MODEL = 1024
A_WIDTH = 512
A_GROUPS = 4
A_BLOCK = 128
B_HEADS = 4
B_HEAD_DIM = 64
B_V_DIM = 128
B_QK = 512
B_VW = 512
C_WIDTH = 512
CONV_K = 3
N_BRANCH = 3
D_FF = 4 * D_MODEL
CHUNK = 64
EPS = 1e-6
LOG2E = math.log2(math.e)
NEG_BIG = -1e30

TM_PROJ = 1024
TQ = 1024
SUB = 256
N_SUB = TQ // SUB
QK_AHEAD = 3
S_SLOTS = 4
POS_RADIX = 32
SLOPE_PIECES = 3
ONES_ROWS = 16
TM_MLP = 512
FF_CHUNK = 1024
PROJ_SUB = 256
N_ATTN_IN = 7
N_PROJ_IN, N_PROJ_OUT = 15, 5
CARRY_ROWS = 8
WT_CHUNK = 256
W_A, W_Q, W_K, W_V, W_C, W_G = 0, 1024, 1536, 2048, 2560, 4096
VMEM_LIMIT = 56 * 1024 * 1024

BF16 = jnp.bfloat16
F32 = jnp.float32


def _dot(a, b):
    return jnp.dot(a, b, preferred_element_type=F32)


def _dot_nt(a, b):
    return lax.dot_general(a, b, (((1,), (1,)), ((), ())), preferred_element_type=F32)


def _rms(x, w):
    return x * lax.rsqrt(jnp.mean(x * x, axis=-1, keepdims=True) + EPS) * w


def _cast_specs(l, stacked, n_steps, step_of):
    in_specs, out_specs, out_shape = [], [], []
    for w in stacked:
        assert w.shape[1] % (16 * n_steps) == 0, "cast blocks must be whole bf16 sublane tiles"
        rows = w.shape[1] // n_steps
        in_specs.append(pl.BlockSpec((None, rows, w.shape[2]), lambda *g: (l, step_of(*g), 0)))
        out_specs.append(pl.BlockSpec((rows, w.shape[2]), lambda *g: (step_of(*g), 0)))
        out_shape.append(jax.ShapeDtypeStruct(w.shape[1:], BF16))
    return in_specs, out_specs, out_shape


def _cast_rows(src_refs, dst_refs):
    for src, dst in zip(src_refs, dst_refs):
        dst[...] = src[...].astype(BF16)


def _proj_kernel(*refs, tiles_per_seq, n_cast):
    (x_ref, n1w_ref, wa_ref, wq_ref, wk_ref, wv_ref, wcb_ref, wcc_ref, wcx_ref,
     vnw_ref, ws_ref, bst_ref, qnw_ref, knw_ref, cw_ref) = refs[:N_PROJ_IN]
    cast_src = refs[N_PROJ_IN:N_PROJ_IN + n_cast]
    ya_ref, qt_ref, k_ref, vt_ref, yc_ref = refs[N_PROJ_IN + n_cast:N_PROJ_IN + n_cast + N_PROJ_OUT]
    cast_dst = refs[N_PROJ_IN + n_cast + N_PROJ_OUT:N_PROJ_IN + 2 * n_cast + N_PROJ_OUT]
    zc_ref, wqvt_ref = refs[N_PROJ_IN + 2 * n_cast + N_PROJ_OUT:]
    i = pl.program_id(0)
    tm = x_ref.shape[0]
    _cast_rows(cast_src, cast_dst)

    @pl.when(i % tiles_per_seq == 0)
    def _():
        zc_ref[...] = jnp.zeros_like(zc_ref)

    @pl.when(i == 0)
    def _():
        for n, w_ref in enumerate((wq_ref, wv_ref)):
            for cb in range(B_QK // WT_CHUNK):
                cols = slice(cb * WT_CHUNK, (cb + 1) * WT_CHUNK)
                wqvt_ref[n * B_QK + cb * WT_CHUNK:n * B_QK + (cb + 1) * WT_CHUNK, :] = (
                    w_ref[:, cols].astype(F32).T.astype(BF16))

    r = lax.broadcasted_iota(jnp.int32, (A_BLOCK, A_BLOCK), 0)
    c = lax.broadcasted_iota(jnp.int32, (A_BLOCK, A_BLOCK), 1)
    gd = A_WIDTH // A_GROUPS
    ws_tril = [jnp.where(c <= r, ws_ref[g], 0.0).astype(BF16) for g in range(A_GROUPS)]
    gr = lax.broadcasted_iota(jnp.int32, (256, 256), 0) // B_HEAD_DIM
    gc = lax.broadcasted_iota(jnp.int32, (256, 256), 1) // B_HEAD_DIM
    ones_bd = jnp.where(gr == gc, 1.0, 0.0).astype(BF16)
    row = lax.broadcasted_iota(jnp.int32, (PROJ_SUB, C_WIDTH), 0)
    z_prev = zc_ref[...]

    def project(sub):
        rs = slice(sub * PROJ_SUB, (sub + 1) * PROJ_SUB)
        h = _rms(x_ref[rs, :], n1w_ref[...]).astype(BF16)
        return dict(pa=_dot(h, wa_ref[...]), pcb=_dot(h, wcb_ref[...]),
                    z=_dot(h, wcc_ref[...]) * _dot(h, wcx_ref[...]),
                    pqv=_dot_nt(wqvt_ref[...], h), pk=_dot(h, wk_ref[...]))

    def finish(sub, p, z_prev):
        rs = slice(sub * PROJ_SUB, (sub + 1) * PROJ_SUB)
        u = jax.nn.gelu(p["pa"][:, :A_WIDTH])
        v = _rms(jax.nn.gelu(p["pa"][:, A_WIDTH:]), vnw_ref[...]).astype(BF16)
        for g in range(A_GROUPS):
            bias = bst_ref[:, g:g + 1]
            for blk in range(PROJ_SUB // A_BLOCK):
                rows = slice(blk * A_BLOCK, (blk + 1) * A_BLOCK)
                cols = slice(g * gd, (g + 1) * gd)
                sv = _dot(ws_tril[g], v[rows, cols]) + bias
                ya_ref[sub * PROJ_SUB + blk * A_BLOCK:sub * PROJ_SUB + (blk + 1) * A_BLOCK, cols] = (
                    u[rows, cols] * sv).astype(BF16)

        z = p["z"]
        last1 = z_prev[CARRY_ROWS - 1:CARRY_ROWS, :]
        last2 = z_prev[CARRY_ROWS - 2:CARRY_ROWS - 1, :]
        z1 = jnp.where(row == 0, last1, pltpu.roll(z, 1, axis=0))
        z2 = jnp.where(row == 0, last2, jnp.where(row == 1, last1, pltpu.roll(z, 2, axis=0)))
        conv = cw_ref[0:1, :] * z2 + cw_ref[1:2, :] * z1 + cw_ref[2:3, :] * z
        yc_ref[rs, :] = (p["pcb"] * conv).astype(BF16)

        pqv = p["pqv"]
        q3 = pqv[:B_QK].reshape(B_QK // B_HEAD_DIM, B_HEAD_DIM, PROJ_SUB)
        qs = jnp.mean(q3 * q3, axis=1, keepdims=True)
        qn = q3 * lax.rsqrt(qs + EPS) * qnw_ref[...].reshape(B_QK // B_HEAD_DIM, B_HEAD_DIM, 1)
        qt_ref[:, rs] = (qn * (B_HEAD_DIM ** -0.5 * LOG2E)).reshape(B_QK, PROJ_SUB).astype(BF16)
        vt_ref[:, rs] = pqv[B_QK:].astype(BF16)

        for j in range(B_QK // 256):
            cols = slice(j * 256, (j + 1) * 256)
            t = p["pk"][:, cols]
            ss = _dot((t * t).astype(BF16), ones_bd) * (1.0 / B_HEAD_DIM)
            k_ref[rs, cols] = (t * lax.rsqrt(ss + EPS) * knw_ref[:, cols]).astype(BF16)
        return z[PROJ_SUB - CARRY_ROWS:, :]

    n_sub = tm // PROJ_SUB
    projected = project(0)
    for sub in range(n_sub):
        following = project(sub + 1) if sub + 1 < n_sub else None
        z_prev = finish(sub, projected, z_prev)
        projected = following

    zc_ref[...] = z_prev


def _layer_block(l, shape, col=0):
    nd = len(shape)
    return pl.BlockSpec((None,) + tuple(shape), lambda *_: (l,) + (0,) * (nd - 1) + (col,),
                        pipeline_mode=pl.Buffered(1))


def _proj_call(l, x2d, n1w, w_in, vnw, ws, bst, qnw, knw, cw, to_cast, *, seq):
    m = x2d.shape[0]
    tm = TM_PROJ
    cast_in, cast_out, cast_shape = _cast_specs(l, to_cast, m // tm, lambda i: i)
    row_bf = lambda width: pl.BlockSpec((tm, width), lambda i: (i, 0))
    col_bf = lambda height: pl.BlockSpec((height, tm), lambda i: (0, i))
    w_cols = lambda width, start: _layer_block(0, (D_MODEL, width), start // width)
    return pl.pallas_call(
        functools.partial(_proj_kernel, tiles_per_seq=seq // tm, n_cast=len(to_cast)),
        grid=(m // tm,),
        in_specs=[
            pl.BlockSpec((tm, D_MODEL), lambda i: (i, 0)),
            _layer_block(l, n1w.shape[1:]),
            w_cols(2 * A_WIDTH, W_A), w_cols(B_QK, W_Q), w_cols(B_QK, W_K), w_cols(B_VW, W_V),
            w_cols(C_WIDTH, W_C), w_cols(C_WIDTH, W_C + C_WIDTH), w_cols(C_WIDTH, W_C + 2 * C_WIDTH),
            _layer_block(l, vnw.shape[1:]), _layer_block(l, ws.shape[1:]), _layer_block(l, bst.shape[1:]),
            _layer_block(l, qnw.shape[1:]), _layer_block(l, knw.shape[1:]), _layer_block(l, cw.shape[1:]),
        ] + cast_in,
        out_specs=[row_bf(A_WIDTH), col_bf(B_QK), row_bf(B_QK), col_bf(B_VW), row_bf(C_WIDTH)] + cast_out,
        out_shape=[
            jax.ShapeDtypeStruct((m, A_WIDTH), BF16),
            jax.ShapeDtypeStruct((B_QK, m), BF16),
            jax.ShapeDtypeStruct((m, B_QK), BF16),
            jax.ShapeDtypeStruct((B_VW, m), BF16),
            jax.ShapeDtypeStruct((m, C_WIDTH), BF16),
        ] + cast_shape,
        scratch_shapes=[pltpu.VMEM((CARRY_ROWS, C_WIDTH), F32),
                        pltpu.VMEM((B_QK + B_VW, D_MODEL), BF16)],
        compiler_params=pltpu.CompilerParams(
            dimension_semantics=("arbitrary",), vmem_limit_bytes=VMEM_LIMIT),
        name="in_proj",
    )(x2d, n1w, w_in, w_in, w_in, w_in, w_in, w_in, w_in, vnw, ws, bst, qnw, knw, cw, *to_cast)


def _attn_kernel(*refs, lambda_init, n_tiles, n_tasks, n_cast):
    slopes_ref, tasks_ref, qt_ref, k_ref, vt_ref, lam_ref, snw_ref = refs[:N_ATTN_IN]
    cast_src = refs[N_ATTN_IN:N_ATTN_IN + n_cast]
    o_ref = refs[N_ATTN_IN + n_cast]
    cast_dst = refs[N_ATTN_IN + n_cast + 1:N_ATTN_IN + 2 * n_cast + 1]
    kfeat_ref, diag_ref, m_ref, l_ref, acc_ref, s_ref = refs[N_ATTN_IN + 2 * n_cast + 1:]
    _cast_rows(cast_src, cast_dst)
    hd = pl.program_id(1)
    slope = slopes_ref[hd] * LOG2E
    n_chains = 2 * N_SUB

    r = lax.broadcasted_iota(jnp.int32, (SUB, SUB), 0)
    c = lax.broadcasted_iota(jnp.int32, (SUB, SUB), 1)
    visible = (r // CHUNK) <= (c // CHUNK)
    diag_ref[...] = jnp.where(visible, jnp.minimum(2.0 * slope * (c - r).astype(F32), 0.0), NEG_BIG)
    kr = lax.broadcasted_iota(jnp.int32, kfeat_ref.shape, 0)
    kl = lax.broadcasted_iota(jnp.int32, kfeat_ref.shape, 1)
    feat = jnp.where(kl < SLOPE_PIECES, kr // POS_RADIX, kr % POS_RADIX)
    kfeat_ref[...] = jnp.where(kl < 2 * SLOPE_PIECES, feat, 0).astype(F32).astype(BF16)

    pieces = []
    rest = jnp.full((2 * B_HEAD_DIM, SUB), slope, F32)
    for _ in range(SLOPE_PIECES):
        piece = rest.astype(BF16).astype(F32)
        pieces.append(piece)
        rest = rest - piece
    frow = lax.broadcasted_iota(jnp.int32, (2 * B_HEAD_DIM, SUB), 0)
    piece = functools.reduce(lambda acc, n: jnp.where(frow % SLOPE_PIECES == n, pieces[n], acc),
                             range(1, SLOPE_PIECES), pieces[0])
    q_feat = jnp.where(frow < SLOPE_PIECES, float(POS_RADIX) * piece,
                       jnp.where(frow < 2 * SLOPE_PIECES, piece, 0.0)).astype(BF16)

    first = frow < B_HEAD_DIM

    def scores(q_start, key_start, ch, nk):
        a, mp = divmod(ch, 2)
        rows = nk * SUB
        kk = jnp.concatenate([k_ref[pl.ds(key_start, rows), :], kfeat_ref[0:rows, :]], axis=1)
        q = qt_ref[:, pl.ds(q_start + a * SUB, SUB)]
        q = jnp.where(first, q, jnp.zeros_like(q)) if mp == 0 else jnp.where(first, jnp.zeros_like(q), q)
        return _dot(kk, jnp.concatenate([q, q_feat], axis=0))

    def softmax_pv(slot, key_start, nk, shift, old, diag_block):
        ts = [s_ref[slot, ks * SUB:(ks + 1) * SUB, :] for ks in range(nk)]
        if diag_block is not None:
            ts[diag_block] = ts[diag_block] + diag_ref[...]
        mb = functools.reduce(jnp.maximum, [jnp.max(t, axis=0, keepdims=True) for t in ts])
        if old is None:
            m_new = mb
        else:
            m_old, l_old, acc_old = old
            m_new = jnp.maximum(m_old, mb + shift)
            alpha = jnp.exp2(m_old - m_new)
        mm = m_new - shift
        pb = jnp.concatenate([jnp.exp2(t - mm).astype(BF16) for t in ts], axis=0)
        vv = jnp.concatenate([vt_ref[:, pl.ds(key_start, nk * SUB)],
                              jnp.ones((ONES_ROWS, nk * SUB), BF16)], axis=0)
        pv_l = _dot(vv, pb)
        pv, lsum = pv_l[:B_V_DIM], pv_l[B_V_DIM:B_V_DIM + 1]
        if old is None:
            return m_new, lsum, pv
        return m_new, alpha * l_old + lsum, alpha * acc_old + pv

    def task_start(t):
        return (pl.multiple_of(tasks_ref[0, t] * TQ, TQ), pl.multiple_of(tasks_ref[1, t] * TQ, TQ))

    diag_chains = [(i, ch) for i in range(n_tiles) for ch in range(n_chains)]
    q0, k0 = task_start(0)

    def diag_scores(pos):
        i, ch = diag_chains[pos]
        nk = ch // 2 + 1
        s_ref[pos % S_SLOTS, 0:nk * SUB, :] = scores(i * TQ, i * TQ, ch, nk)

    for pos in range(QK_AHEAD):
        diag_scores(pos)
    for pos, (i, ch) in enumerate(diag_chains):
        ahead = pos + QK_AHEAD
        if ahead < len(diag_chains):
            diag_scores(ahead)
        else:
            s_ref[ahead % S_SLOTS] = scores(q0, k0, ahead - len(diag_chains), N_SUB)
        a = ch // 2
        m_new, l_new, acc_new = softmax_pv(pos % S_SLOTS, i * TQ, a + 1, 0.0, None, a)
        m_ref[i * n_chains + ch] = m_new
        l_ref[i * n_chains + ch] = l_new
        acc_ref[i * n_chains + ch] = acc_new

    def off_diag(t, carry):
        q_start, key_start = task_start(t)
        nq_start, nkey_start = task_start(jnp.minimum(t + 1, n_tasks - 1))
        base = tasks_ref[0, t] * n_chains
        shift = slope * (key_start - q_start).astype(F32)
        olds = [(m_ref[base + ch], l_ref[base + ch], acc_ref[base + ch]) for ch in range(n_chains)]
        news = []
        for ch in range(n_chains):
            ahead = ch + QK_AHEAD
            if ahead < n_chains:
                s_ref[ahead % S_SLOTS] = scores(q_start, key_start, ahead, N_SUB)
            else:
                s_ref[ahead % S_SLOTS] = scores(nq_start, nkey_start, ahead - n_chains, N_SUB)
            news.append(softmax_pv(ch % S_SLOTS, key_start, N_SUB, shift, olds[ch], None))
        for ch, (m_new, l_new, acc_new) in enumerate(news):
            m_ref[base + ch] = m_new
            l_ref[base + ch] = l_new
            acc_ref[base + ch] = acc_new
        return carry

    lp = lam_ref[...]
    lam = (jnp.exp(jnp.sum(lp[0:1] * lp[1:2], axis=-1, keepdims=True))
           - jnp.exp(jnp.sum(lp[2:3] * lp[3:4], axis=-1, keepdims=True)) + lambda_init)
    eye = jnp.where(r == c, 1.0, 0.0).astype(BF16)

    def finish_blocks(blocks):
        for blk in blocks:
            ot = acc_ref[2 * blk] / l_ref[2 * blk] - lam * (acc_ref[2 * blk + 1] / l_ref[2 * blk + 1])
            inv = lax.rsqrt(jnp.mean(ot * ot, axis=0, keepdims=True) + EPS)
            on = (ot * inv * snw_ref[...] * (1.0 - lambda_init)).astype(BF16)
            o_ref[blk * SUB:(blk + 1) * SUB, :] = _dot_nt(eye, on).astype(BF16)

    finish_blocks(range(N_SUB))
    lax.fori_loop(0, n_tasks, off_diag, 0)
    finish_blocks(range(N_SUB, n_tiles * N_SUB))


def _attn_call(l, slopes, qt, k, vt, lam_p, snw, to_cast, *, batch, seq, lambda_init):
    m = k.shape[0]
    cast_in, cast_out, cast_shape = _cast_specs(l + 1, to_cast, batch * B_HEADS, lambda b, h: b * B_HEADS + h)
    n_tiles = seq // TQ
    pairs = [(i, j) for i in range(n_tiles) for j in range(i)]
    tasks = jnp.array([[i for i, _ in pairs], [j for _, j in pairs]], jnp.int32)
    n_chains_total = n_tiles * 2 * N_SUB
    assert n_tiles >= 2 and QK_AHEAD < S_SLOTS and (2 * N_SUB) % S_SLOTS == 0
    smem = pl.BlockSpec(memory_space=pltpu.SMEM)
    return pl.pallas_call(
        functools.partial(_attn_kernel, lambda_init=lambda_init, n_tiles=n_tiles, n_tasks=len(pairs),
                          n_cast=len(to_cast)),
        grid=(batch, B_HEADS),
        in_specs=[
            smem, smem,
            pl.BlockSpec((B_V_DIM, seq), lambda b, h: (h, b)),
            pl.BlockSpec((seq, B_V_DIM), lambda b, h: (b, h)),
            pl.BlockSpec((B_V_DIM, seq), lambda b, h: (h, b)),
            pl.BlockSpec((None,) + lam_p.shape[1:], lambda b, h: (l, 0, 0)),
            pl.BlockSpec((None,) + snw.shape[1:], lambda b, h: (l, 0, 0)),
        ] + cast_in,
        out_specs=[pl.BlockSpec((seq, B_V_DIM), lambda b, h: (b, h))] + cast_out,
        out_shape=[jax.ShapeDtypeStruct((m, B_VW), BF16)] + cast_shape,
        scratch_shapes=[
            pltpu.VMEM((TQ, 2 * B_HEAD_DIM), BF16),
            pltpu.VMEM((SUB, SUB), F32),
            pltpu.VMEM((n_chains_total, 1, SUB), F32),
            pltpu.VMEM((n_chains_total, 1, SUB), F32),
            pltpu.VMEM((n_chains_total, B_V_DIM, SUB), F32),
            pltpu.VMEM((S_SLOTS, TQ, SUB), F32),
        ],
        compiler_params=pltpu.CompilerParams(
            dimension_semantics=("arbitrary", "arbitrary"), vmem_limit_bytes=VMEM_LIMIT),
        name="diff_attn",
    )(slopes, tasks, qt, k, vt, lam_p, snw, *to_cast)


def _mlp_kernel(x_ref, ya_ref, yb_ref, yc_ref, n1w_ref, wg0_ref, wg1_ref, wg2_ref, gb_ref,
                wba_ref, wbb_ref, wbc_ref, wo_ref, n2w_ref, w1_ref, w2_ref, o_ref):
    x = x_ref[...]
    h = _rms(x, n1w_ref[...]).astype(BF16)
    merged = None
    branches = ((ya_ref, wba_ref, wg0_ref), (yb_ref, wbb_ref, wg1_ref), (yc_ref, wbc_ref, wg2_ref))
    for br, (y_ref, w_ref, wg_ref) in enumerate(branches):
        gate = jax.nn.sigmoid(_dot(h, wg_ref[...]) + gb_ref[br:br + 1, :])
        term = gate * _dot(y_ref[...], w_ref[...])
        merged = term if merged is None else merged + term
    x1 = x + _dot(merged.astype(BF16), wo_ref[...])
    h2 = _rms(x1, n2w_ref[...]).astype(BF16)
    acc = x1
    for cidx in range(D_FF // FF_CHUNK):
        cols = slice(cidx * FF_CHUNK, (cidx + 1) * FF_CHUNK)
        f = jnp.square(jnp.maximum(_dot(h2, w1_ref[:, cols]), 0.0)).astype(BF16)
        acc = acc + _dot(f, w2_ref[cols, :])
    o_ref[...] = acc


def _mlp_call(l, x2d, ya, yb, yc, n1w, w_in, gb, wba, wbb, wbc, wo, n2w, w1, w2):
    m = x2d.shape[0]
    tm = TM_MLP
    rows = lambda width: pl.BlockSpec((tm, width), lambda i: (i, 0))
    whole = lambda arr: _layer_block(l, arr.shape[1:])
    own = lambda arr: _layer_block(0, arr.shape[1:])
    gate_cols = lambda br: _layer_block(0, (D_MODEL, D_MODEL), W_G // D_MODEL + br)
    return pl.pallas_call(
        _mlp_kernel,
        grid=(m // tm,),
        in_specs=[rows(D_MODEL), rows(A_WIDTH), rows(B_VW), rows(C_WIDTH),
                  whole(n1w), gate_cols(0), gate_cols(1), gate_cols(2), whole(gb),
                  own(wba), own(wbb), own(wbc), own(wo), whole(n2w), own(w1), own(w2)],
        out_specs=rows(D_MODEL),
        out_shape=jax.ShapeDtypeStruct((m, D_MODEL), F32),
        compiler_params=pltpu.CompilerParams(
            dimension_semantics=("arbitrary",), vmem_limit_bytes=VMEM_LIMIT),
        name="merge_mlp",
    )(x2d, ya, yb, yc, n1w, w_in, w_in, w_in, gb, wba, wbb, wbc, wo, n2w, w1, w2)


def kernel(x, norm1_w, w_in, gate_b, a_vnorm_w, a_ws, a_bs, b_qnorm_w, b_knorm_w, b_lam, b_subnorm_w,
           c_conv_w, w_br_a, w_br_b, w_br_c, w_out, norm2_w, w_ff1, w_ff2):
    batch, seq, d = x.shape
    depth = w_in.shape[0]
    slopes = 2.0 ** (-8.0 * jnp.arange(1, B_HEADS + 1, dtype=F32) / B_HEADS)
    n1w = norm1_w.reshape(depth, 1, d)
    n2w = norm2_w.reshape(depth, 1, d)
    vnw = a_vnorm_w.reshape(depth, 1, A_WIDTH)
    bst = a_bs.transpose(0, 2, 1)
    qnw = jnp.tile(b_qnorm_w.reshape(depth, 1, 2 * B_HEAD_DIM), (1, B_HEADS, 1)).reshape(depth, B_QK, 1)
    knw = jnp.tile(b_knorm_w.reshape(depth, 1, 2 * B_HEAD_DIM), (1, B_HEADS, 1)).reshape(depth, 1, B_QK)
    snw = b_subnorm_w.reshape(depth, B_V_DIM, 1)
    later_weights = (w_br_a, w_br_b, w_br_c, w_out, w_ff1, w_ff2)
    x2d = x.reshape(batch * seq, d)
    w_in_l = w_in[0].astype(BF16)
    for l in range(depth):
        lambda_init = 0.8 - 0.6 * math.exp(-0.3 * l)
        w_in_l = w_in_l[None]
        ya, qt, k, vt, yc, *later_b = _proj_call(
            l, x2d, n1w, w_in_l, vnw, a_ws, bst, qnw, knw, c_conv_w, later_weights, seq=seq)
        yb, *next_w_in = _attn_call(l, slopes, qt, k, vt, b_lam, snw, (w_in,) if l + 1 < depth else (),
                                    batch=batch, seq=seq, lambda_init=lambda_init)
        wba, wbb, wbc, wo, w1, w2 = (w[None] for w in later_b)
        x2d = _mlp_call(l, x2d, ya, yb, yc, n1w, w_in_l, gate_b, wba, wbb, wbc, wo, n2w, w1, w2)
        if next_w_in:
            w_in_l = next_w_in[0]
    return x2d.reshape(batch, seq, d)
```

```python
import functools
import math

import jax
import jax.numpy as jnp
from jax import lax
from jax.experimental import pallas as pl
from jax.experimental.pallas import tpu as pltpu

D_MODEL = 1024
A_WIDTH = 512
A_GROUPS = 4
A_BLOCK = 128
B_HEADS = 4
B_HEAD_DIM = 64
B_V_DIM = 128
B_QK = 512
B_VW = 512
C_WIDTH = 512
CONV_K = 3
N_BRANCH = 3
D_FF = 4 * D_MODEL
CHUNK = 64
EPS = 1e-6
LOG2E = math.log2(math.e)
NEG_BIG = -1e30

TM_PROJ = 1024
TQ = 1024
SUB = 256
N_SUB = TQ // SUB
QK_AHEAD = 3
S_SLOTS = 4
POS_RADIX = 32
SLOPE_PIECES = 3
ONES_ROWS = 16
TM_MLP = 512
MLP_SUB = 256
FF_CHUNK = 1024
PROJ_SUB = 256
PROJ_LAG = 2
N_ATTN_IN = 7
N_PROJ_IN, N_PROJ_OUT = 15, 5
CARRY_ROWS = 8
WT_CHUNK = 256
W_A, W_Q, W_K, W_V, W_C, W_G = 0, 1024, 1536, 2048, 2560, 4096
VMEM_LIMIT = 56 * 1024 * 1024

BF16 = jnp.bfloat16
F32 = jnp.float32


def _dot(a, b):
    return jnp.dot(a, b, preferred_element_type=F32)


def _dot_nt(a, b):
    return lax.dot_general(a, b, (((1,), (1,)), ((), ())), preferred_element_type=F32)


def _rms(x, w):
    return x * lax.rsqrt(jnp.mean(x * x, axis=-1, keepdims=True) + EPS) * w


def _cast_specs(l, stacked, n_steps, step_of):
    in_specs, out_specs, out_shape = [], [], []
    for w in stacked:
        assert w.shape[1] % (16 * n_steps) == 0, "cast blocks must be whole bf16 sublane tiles"
        rows = w.shape[1] // n_steps
        in_specs.append(pl.BlockSpec((None, rows, w.shape[2]), lambda *g: (l, step_of(*g), 0)))
        out_specs.append(pl.BlockSpec((rows, w.shape[2]), lambda *g: (step_of(*g), 0)))
        out_shape.append(jax.ShapeDtypeStruct(w.shape[1:], BF16))
    return in_specs, out_specs, out_shape


def _cast_rows(src_refs, dst_refs):
    for src, dst in zip(src_refs, dst_refs):
        dst[...] = src[...].astype(BF16)


def _proj_kernel(*refs, tiles_per_seq, n_cast):
    (x_ref, n1w_ref, wa_ref, wq_ref, wk_ref, wv_ref, wcb_ref, wcc_ref, wcx_ref,
     vnw_ref, ws_ref, bst_ref, qnw_ref, knw_ref, cw_ref) = refs[:N_PROJ_IN]
    cast_src = refs[N_PROJ_IN:N_PROJ_IN + n_cast]
    ya_ref, qt_ref, k_ref, vt_ref, yc_ref = refs[N_PROJ_IN + n_cast:N_PROJ_IN + n_cast + N_PROJ_OUT]
    cast_dst = refs[N_PROJ_IN + n_cast + N_PROJ_OUT:N_PROJ_IN + 2 * n_cast + N_PROJ_OUT]
    zc_ref, wqvt_ref = refs[N_PROJ_IN + 2 * n_cast + N_PROJ_OUT:]
    i = pl.program_id(0)
    tm = x_ref.shape[0]
    _cast_rows(cast_src, cast_dst)

    @pl.when(i % tiles_per_seq == 0)
    def _():
        zc_ref[...] = jnp.zeros_like(zc_ref)

    @pl.when(i == 0)
    def _():
        for n, w_ref in enumerate((wq_ref, wv_ref)):
            for cb in range(B_QK // WT_CHUNK):
                cols = slice(cb * WT_CHUNK, (cb + 1) * WT_CHUNK)
                wqvt_ref[n * B_QK + cb * WT_CHUNK:n * B_QK + (cb + 1) * WT_CHUNK, :] = (
                    w_ref[:, cols].astype(F32).T.astype(BF16))

    r = lax.broadcasted_iota(jnp.int32, (A_BLOCK, A_BLOCK), 0)
    c = lax.broadcasted_iota(jnp.int32, (A_BLOCK, A_BLOCK), 1)
    gd = A_WIDTH // A_GROUPS
    ws_tril = [jnp.where(c <= r, ws_ref[g], 0.0).astype(BF16) for g in range(A_GROUPS)]
    gr = lax.broadcasted_iota(jnp.int32, (256, 256), 0) // B_HEAD_DIM
    gc = lax.broadcasted_iota(jnp.int32, (256, 256), 1) // B_HEAD_DIM
    ones_bd = jnp.where(gr == gc, 1.0, 0.0).astype(BF16)
    row = lax.broadcasted_iota(jnp.int32, (PROJ_SUB, C_WIDTH), 0)
    z_prev = zc_ref[...]

    def project(sub):
        rs = slice(sub * PROJ_SUB, (sub + 1) * PROJ_SUB)
        h = _rms(x_ref[rs, :], n1w_ref[...]).astype(BF16)
        return dict(pa=_dot(h, wa_ref[...]), pcb=_dot(h, wcb_ref[...]),
                    z=_dot(h, wcc_ref[...]) * _dot(h, wcx_ref[...]),
                    pqv=_dot_nt(wqvt_ref[...], h), pk=_dot(h, wk_ref[...]))

    def finish(sub, p, z_prev):
        rs = slice(sub * PROJ_SUB, (sub + 1) * PROJ_SUB)
        u = jax.nn.gelu(p["pa"][:, :A_WIDTH])
        v = _rms(jax.nn.gelu(p["pa"][:, A_WIDTH:]), vnw_ref[...]).astype(BF16)
        for g in range(A_GROUPS):
            bias = bst_ref[:, g:g + 1]
            for blk in range(PROJ_SUB // A_BLOCK):
                rows = slice(blk * A_BLOCK, (blk + 1) * A_BLOCK)
                cols = slice(g * gd, (g + 1) * gd)
                sv = _dot(ws_tril[g], v[rows, cols]) + bias
                ya_ref[sub * PROJ_SUB + blk * A_BLOCK:sub * PROJ_SUB + (blk + 1) * A_BLOCK, cols] = (
                    u[rows, cols] * sv).astype(BF16)

        z = p["z"]
        last1 = z_prev[CARRY_ROWS - 1:CARRY_ROWS, :]
        last2 = z_prev[CARRY_ROWS - 2:CARRY_ROWS - 1, :]
        z1 = jnp.where(row == 0, last1, pltpu.roll(z, 1, axis=0))
        z2 = jnp.where(row == 0, last2, jnp.where(row == 1, last1, pltpu.roll(z, 2, axis=0)))
        conv = cw_ref[0:1, :] * z2 + cw_ref[1:2, :] * z1 + cw_ref[2:3, :] * z
        yc_ref[rs, :] = (p["pcb"] * conv).astype(BF16)

        pqv = p["pqv"]
        q3 = pqv[:B_QK].reshape(B_QK // B_HEAD_DIM, B_HEAD_DIM, PROJ_SUB)
        qs = jnp.mean(q3 * q3, axis=1, keepdims=True)
        qn = q3 * lax.rsqrt(qs + EPS) * qnw_ref[...].reshape(B_QK // B_HEAD_DIM, B_HEAD_DIM, 1)
        qt_ref[:, rs] = (qn * (B_HEAD_DIM ** -0.5 * LOG2E)).reshape(B_QK, PROJ_SUB).astype(BF16)
        vt_ref[:, rs] = pqv[B_QK:].astype(BF16)

        for j in range(B_QK // 256):
            cols = slice(j * 256, (j + 1) * 256)
            t = p["pk"][:, cols]
            ss = _dot((t * t).astype(BF16), ones_bd) * (1.0 / B_HEAD_DIM)
            k_ref[rs, cols] = (t * lax.rsqrt(ss + EPS) * knw_ref[:, cols]).astype(BF16)
        return z[PROJ_SUB - CARRY_ROWS:, :]

    n_sub = tm // PROJ_SUB
    projected = [project(sub) for sub in range(min(PROJ_LAG, n_sub))]
    for sub in range(n_sub):
        if sub + PROJ_LAG < n_sub:
            projected.append(project(sub + PROJ_LAG))
        z_prev = finish(sub, projected.pop(0), z_prev)

    zc_ref[...] = z_prev


def _layer_block(l, shape, col=0):
    nd = len(shape)
    return pl.BlockSpec((None,) + tuple(shape), lambda *_: (l,) + (0,) * (nd - 1) + (col,),
                        pipeline_mode=pl.Buffered(1))


def _proj_call(l, x2d, n1w, w_in, vnw, ws, bst, qnw, knw, cw, to_cast, *, seq):
    m = x2d.shape[0]
    tm = TM_PROJ
    cast_in, cast_out, cast_shape = _cast_specs(l, to_cast, m // tm, lambda i: i)
    row_bf = lambda width: pl.BlockSpec((tm, width), lambda i: (i, 0))
    col_bf = lambda height: pl.BlockSpec((height, tm), lambda i: (0, i))
    w_cols = lambda width, start: _layer_block(0, (D_MODEL, width), start // width)
    return pl.pallas_call(
        functools.partial(_proj_kernel, tiles_per_seq=seq // tm, n_cast=len(to_cast)),
        grid=(m // tm,),
        in_specs=[
            pl.BlockSpec((tm, D_MODEL), lambda i: (i, 0)),
            _layer_block(l, n1w.shape[1:]),
            w_cols(2 * A_WIDTH, W_A), w_cols(B_QK, W_Q), w_cols(B_QK, W_K), w_cols(B_VW, W_V),
            w_cols(C_WIDTH, W_C), w_cols(C_WIDTH, W_C + C_WIDTH), w_cols(C_WIDTH, W_C + 2 * C_WIDTH),
            _layer_block(l, vnw.shape[1:]), _layer_block(l, ws.shape[1:]), _layer_block(l, bst.shape[1:]),
            _layer_block(l, qnw.shape[1:]), _layer_block(l, knw.shape[1:]), _layer_block(l, cw.shape[1:]),
        ] + cast_in,
        out_specs=[row_bf(A_WIDTH), col_bf(B_QK), row_bf(B_QK), col_bf(B_VW), row_bf(C_WIDTH)] + cast_out,
        out_shape=[
            jax.ShapeDtypeStruct((m, A_WIDTH), BF16),
            jax.ShapeDtypeStruct((B_QK, m), BF16),
            jax.ShapeDtypeStruct((m, B_QK), BF16),
            jax.ShapeDtypeStruct((B_VW, m), BF16),
            jax.ShapeDtypeStruct((m, C_WIDTH), BF16),
        ] + cast_shape,
        scratch_shapes=[pltpu.VMEM((CARRY_ROWS, C_WIDTH), F32),
                        pltpu.VMEM((B_QK + B_VW, D_MODEL), BF16)],
        compiler_params=pltpu.CompilerParams(
            dimension_semantics=("arbitrary",), vmem_limit_bytes=VMEM_LIMIT),
        name="in_proj",
    )(x2d, n1w, w_in, w_in, w_in, w_in, w_in, w_in, w_in, vnw, ws, bst, qnw, knw, cw, *to_cast)


def _attn_kernel(*refs, lambda_init, n_tiles, n_tasks, n_cast):
    slopes_ref, tasks_ref, qt_ref, k_ref, vt_ref, lam_ref, snw_ref = refs[:N_ATTN_IN]
    cast_src = refs[N_ATTN_IN:N_ATTN_IN + n_cast]
    o_ref = refs[N_ATTN_IN + n_cast]
    cast_dst = refs[N_ATTN_IN + n_cast + 1:N_ATTN_IN + 2 * n_cast + 1]
    kfeat_ref, diag_ref, m_ref, l_ref, acc_ref, s_ref = refs[N_ATTN_IN + 2 * n_cast + 1:]
    _cast_rows(cast_src, cast_dst)
    hd = pl.program_id(1)
    slope = slopes_ref[hd] * LOG2E
    n_chains = 2 * N_SUB

    r = lax.broadcasted_iota(jnp.int32, (SUB, SUB), 0)
    c = lax.broadcasted_iota(jnp.int32, (SUB, SUB), 1)
    visible = (r // CHUNK) <= (c // CHUNK)
    diag_ref[...] = jnp.where(visible, jnp.minimum(2.0 * slope * (c - r).astype(F32), 0.0), NEG_BIG)
    kr = lax.broadcasted_iota(jnp.int32, kfeat_ref.shape, 0)
    kl = lax.broadcasted_iota(jnp.int32, kfeat_ref.shape, 1)
    feat = jnp.where(kl < SLOPE_PIECES, kr // POS_RADIX, kr % POS_RADIX)
    kfeat_ref[...] = jnp.where(kl < 2 * SLOPE_PIECES, feat, 0).astype(F32).astype(BF16)

    pieces = []
    rest = jnp.full((2 * B_HEAD_DIM, SUB), slope, F32)
    for _ in range(SLOPE_PIECES):
        piece = rest.astype(BF16).astype(F32)
        pieces.append(piece)
        rest = rest - piece
    frow = lax.broadcasted_iota(jnp.int32, (2 * B_HEAD_DIM, SUB), 0)
    piece = functools.reduce(lambda acc, n: jnp.where(frow % SLOPE_PIECES == n, pieces[n], acc),
                             range(1, SLOPE_PIECES), pieces[0])
    q_feat = jnp.where(frow < SLOPE_PIECES, float(POS_RADIX) * piece,
                       jnp.where(frow < 2 * SLOPE_PIECES, piece, 0.0)).astype(BF16)

    first = frow < B_HEAD_DIM

    def scores(q_start, key_start, ch, nk):
        a, mp = divmod(ch, 2)
        rows = nk * SUB
        kk = jnp.concatenate([k_ref[pl.ds(key_start, rows), :], kfeat_ref[0:rows, :]], axis=1)
        q = qt_ref[:, pl.ds(q_start + a * SUB, SUB)]
        q = jnp.where(first, q, jnp.zeros_like(q)) if mp == 0 else jnp.where(first, jnp.zeros_like(q), q)
        return _dot(kk, jnp.concatenate([q, q_feat], axis=0))

    def softmax_pv(slot, key_start, nk, shift, old, diag_block):
        ts = [s_ref[slot, ks * SUB:(ks + 1) * SUB, :] for ks in range(nk)]
        if diag_block is not None:
            ts[diag_block] = ts[diag_block] + diag_ref[...]
        mb = functools.reduce(jnp.maximum, [jnp.max(t, axis=0, keepdims=True) for t in ts])
        if old is None:
            m_new = mb
        else:
            m_old, l_old, acc_old = old
            m_new = jnp.maximum(m_old, mb + shift)
            alpha = jnp.exp2(m_old - m_new)
        mm = m_new - shift
        pb = jnp.concatenate([jnp.exp2(t - mm).astype(BF16) for t in ts], axis=0)
        vv = jnp.concatenate([vt_ref[:, pl.ds(key_start, nk * SUB)],
                              jnp.ones((ONES_ROWS, nk * SUB), BF16)], axis=0)
        pv_l = _dot(vv, pb)
        pv, lsum = pv_l[:B_V_DIM], pv_l[B_V_DIM:B_V_DIM + 1]
        if old is None:
            return m_new, lsum, pv
        return m_new, alpha * l_old + lsum, alpha * acc_old + pv

    def task_start(t):
        return (pl.multiple_of(tasks_ref[0, t] * TQ, TQ), pl.multiple_of(tasks_ref[1, t] * TQ, TQ))

    diag_chains = [(i, ch) for i in range(n_tiles) for ch in range(n_chains)]
    q0, k0 = task_start(0)

    def diag_scores(pos):
        i, ch = diag_chains[pos]
        nk = ch // 2 + 1
        s_ref[pos % S_SLOTS, 0:nk * SUB, :] = scores(i * TQ, i * TQ, ch, nk)

    for pos in range(QK_AHEAD):
        diag_scores(pos)
    for pos, (i, ch) in enumerate(diag_chains):
        ahead = pos + QK_AHEAD
        if ahead < len(diag_chains):
            diag_scores(ahead)
        else:
            s_ref[ahead % S_SLOTS] = scores(q0, k0, ahead - len(diag_chains), N_SUB)
        a = ch // 2
        m_new, l_new, acc_new = softmax_pv(pos % S_SLOTS, i * TQ, a + 1, 0.0, None, a)
        m_ref[i * n_chains + ch] = m_new
        l_ref[i * n_chains + ch] = l_new
        acc_ref[i * n_chains + ch] = acc_new

    def off_diag(t, carry):
        q_start, key_start = task_start(t)
        nq_start, nkey_start = task_start(jnp.minimum(t + 1, n_tasks - 1))
        base = tasks_ref[0, t] * n_chains
        shift = slope * (key_start - q_start).astype(F32)
        olds = [(m_ref[base + ch], l_ref[base + ch], acc_ref[base + ch]) for ch in range(n_chains)]
        news = []
        for ch in range(n_chains):
            ahead = ch + QK_AHEAD
            if ahead < n_chains:
                s_ref[ahead % S_SLOTS] = scores(q_start, key_start, ahead, N_SUB)
            else:
                s_ref[ahead % S_SLOTS] = scores(nq_start, nkey_start, ahead - n_chains, N_SUB)
            news.append(softmax_pv(ch % S_SLOTS, key_start, N_SUB, shift, olds[ch], None))
        for ch, (m_new, l_new, acc_new) in enumerate(news):
            m_ref[base + ch] = m_new
            l_ref[base + ch] = l_new
            acc_ref[base + ch] = acc_new
        return carry

    lp = lam_ref[...]
    lam = (jnp.exp(jnp.sum(lp[0:1] * lp[1:2], axis=-1, keepdims=True))
           - jnp.exp(jnp.sum(lp[2:3] * lp[3:4], axis=-1, keepdims=True)) + lambda_init)
    eye = jnp.where(r == c, 1.0, 0.0).astype(BF16)

    def finish_blocks(blocks):
        for blk in blocks:
            ot = acc_ref[2 * blk] / l_ref[2 * blk] - lam * (acc_ref[2 * blk + 1] / l_ref[2 * blk + 1])
            inv = lax.rsqrt(jnp.mean(ot * ot, axis=0, keepdims=True) + EPS)
            on = (ot * inv * snw_ref[...] * (1.0 - lambda_init)).astype(BF16)
            o_ref[blk * SUB:(blk + 1) * SUB, :] = _dot_nt(eye, on).astype(BF16)

    finish_blocks(range(N_SUB))
    lax.fori_loop(0, n_tasks, off_diag, 0)
    finish_blocks(range(N_SUB, n_tiles * N_SUB))


def _attn_call(l, slopes, qt, k, vt, lam_p, snw, to_cast, *, batch, seq, lambda_init):
    m = k.shape[0]
    cast_in, cast_out, cast_shape = _cast_specs(l + 1, to_cast, batch * B_HEADS, lambda b, h: b * B_HEADS + h)
    n_tiles = seq // TQ
    pairs = [(i, j) for i in range(n_tiles) for j in range(i)]
    tasks = jnp.array([[i for i, _ in pairs], [j for _, j in pairs]], jnp.int32)
    n_chains_total = n_tiles * 2 * N_SUB
    assert n_tiles >= 2 and QK_AHEAD < S_SLOTS and (2 * N_SUB) % S_SLOTS == 0
    smem = pl.BlockSpec(memory_space=pltpu.SMEM)
    return pl.pallas_call(
        functools.partial(_attn_kernel, lambda_init=lambda_init, n_tiles=n_tiles, n_tasks=len(pairs),
                          n_cast=len(to_cast)),
        grid=(batch, B_HEADS),
        in_specs=[
            smem, smem,
            pl.BlockSpec((B_V_DIM, seq), lambda b, h: (h, b)),
            pl.BlockSpec((seq, B_V_DIM), lambda b, h: (b, h)),
            pl.BlockSpec((B_V_DIM, seq), lambda b, h: (h, b)),
            pl.BlockSpec((None,) + lam_p.shape[1:], lambda b, h: (l, 0, 0)),
            pl.BlockSpec((None,) + snw.shape[1:], lambda b, h: (l, 0, 0)),
        ] + cast_in,
        out_specs=[pl.BlockSpec((seq, B_V_DIM), lambda b, h: (b, h))] + cast_out,
        out_shape=[jax.ShapeDtypeStruct((m, B_VW), BF16)] + cast_shape,
        scratch_shapes=[
            pltpu.VMEM((TQ, 2 * B_HEAD_DIM), BF16),
            pltpu.VMEM((SUB, SUB), F32),
            pltpu.VMEM((n_chains_total, 1, SUB), F32),
            pltpu.VMEM((n_chains_total, 1, SUB), F32),
            pltpu.VMEM((n_chains_total, B_V_DIM, SUB), F32),
            pltpu.VMEM((S_SLOTS, TQ, SUB), F32),
        ],
        compiler_params=pltpu.CompilerParams(
            dimension_semantics=("arbitrary", "arbitrary"), vmem_limit_bytes=VMEM_LIMIT),
        name="diff_attn",
    )(slopes, tasks, qt, k, vt, lam_p, snw, *to_cast)


def _mlp_kernel(x_ref, ya_ref, yb_ref, yc_ref, n1w_ref, wg0_ref, wg1_ref, wg2_ref, gb_ref,
                wba_ref, wbb_ref, wbc_ref, wo_ref, n2w_ref, w1_ref, w2_ref, o_ref):
    branches = ((ya_ref, wba_ref, wg0_ref), (yb_ref, wbb_ref, wg1_ref), (yc_ref, wbc_ref, wg2_ref))

    def mix(rs):
        x = x_ref[rs, :]
        h = _rms(x, n1w_ref[...]).astype(BF16)
        merged = None
        for br, (y_ref, w_ref, wg_ref) in enumerate(branches):
            gate = jax.nn.sigmoid(_dot(h, wg_ref[...]) + gb_ref[br:br + 1, :])
            term = gate * _dot(y_ref[rs, :], w_ref[...])
            merged = term if merged is None else merged + term
        return x + _dot(merged.astype(BF16), wo_ref[...])

    def mlp(rs, x1):
        h2 = _rms(x1, n2w_ref[...]).astype(BF16)
        acc = x1
        for cidx in range(D_FF // FF_CHUNK):
            cols = slice(cidx * FF_CHUNK, (cidx + 1) * FF_CHUNK)
            f = jnp.square(jnp.maximum(_dot(h2, w1_ref[:, cols]), 0.0)).astype(BF16)
            acc = acc + _dot(f, w2_ref[cols, :])
        o_ref[rs, :] = acc

    pieces = [slice(s * MLP_SUB, (s + 1) * MLP_SUB) for s in range(x_ref.shape[0] // MLP_SUB)]
    mixed = [mix(rs) for rs in pieces]
    for rs, x1 in zip(pieces, mixed):
        mlp(rs, x1)


def _mlp_call(l, x2d, ya, yb, yc, n1w, w_in, gb, wba, wbb, wbc, wo, n2w, w1, w2):
    m = x2d.shape[0]
    tm = TM_MLP
    rows = lambda width: pl.BlockSpec((tm, width), lambda i: (i, 0))
    whole = lambda arr: _layer_block(l, arr.shape[1:])
    own = lambda arr: _layer_block(0, arr.shape[1:])
    gate_cols = lambda br: _layer_block(0, (D_MODEL, D_MODEL), W_G // D_MODEL + br)
    return pl.pallas_call(
        _mlp_kernel,
        grid=(m // tm,),
        in_specs=[rows(D_MODEL), rows(A_WIDTH), rows(B_VW), rows(C_WIDTH),
                  whole(n1w), gate_cols(0), gate_cols(1), gate_cols(2), whole(gb),
                  own(wba), own(wbb), own(wbc), own(wo), whole(n2w), own(w1), own(w2)],
        out_specs=rows(D_MODEL),
        out_shape=jax.ShapeDtypeStruct((m, D_MODEL), F32),
        compiler_params=pltpu.CompilerParams(
            dimension_semantics=("arbitrary",), vmem_limit_bytes=VMEM_LIMIT),
        name="merge_mlp",
    )(x2d, ya, yb, yc, n1w, w_in, w_in, w_in, gb, wba, wbb, wbc, wo, n2w, w1, w2)


def kernel(x, norm1_w, w_in, gate_b, a_vnorm_w, a_ws, a_bs, b_qnorm_w, b_knorm_w, b_lam, b_subnorm_w,
           c_conv_w, w_br_a, w_br_b, w_br_c, w_out, norm2_w, w_ff1, w_ff2):
    batch, seq, d = x.shape
    depth = w_in.shape[0]
    slopes = 2.0 ** (-8.0 * jnp.arange(1, B_HEADS + 1, dtype=F32) / B_HEADS)
    n1w = norm1_w.reshape(depth, 1, d)
    n2w = norm2_w.reshape(depth, 1, d)
    vnw = a_vnorm_w.reshape(depth, 1, A_WIDTH)
    bst = a_bs.transpose(0, 2, 1)
    qnw = jnp.tile(b_qnorm_w.reshape(depth, 1, 2 * B_HEAD_DIM), (1, B_HEADS, 1)).reshape(depth, B_QK, 1)
    knw = jnp.tile(b_knorm_w.reshape(depth, 1, 2 * B_HEAD_DIM), (1, B_HEADS, 1)).reshape(depth, 1, B_QK)
    snw = b_subnorm_w.reshape(depth, B_V_DIM, 1)
    later_weights = (w_br_a, w_br_b, w_br_c, w_out, w_ff1, w_ff2)
    x2d = x.reshape(batch * seq, d)
    w_in_l = w_in[0].astype(BF16)
    for l in range(depth):
        lambda_init = 0.8 - 0.6 * math.exp(-0.3 * l)
        w_in_l = w_in_l[None]
        ya, qt, k, vt, yc, *later_b = _proj_call(
            l, x2d, n1w, w_in_l, vnw, a_ws, bst, qnw, knw, c_conv_w, later_weights, seq=seq)
        yb, *next_w_in = _attn_call(l, slopes, qt, k, vt, b_lam, snw, (w_in,) if l + 1 < depth else (),
                                    batch=batch, seq=seq, lambda_init=lambda_init)
        wba, wbb, wbc, wo, w1, w2 = (w[None] for w in later_b)
        x2d = _mlp_call(l, x2d, ya, yb, yc, n1w, w_in_l, gate_b, wba, wbb, wbc, wo, n2w, w1, w2)
        if next_w_in:
            w_in_l = next_w_in[0]
    return x2d.reshape(batch, seq, d)
```
